```python
import jax, jax.numpy as jnp
from jax import lax
import numpy as np

D_MODEL = 4096
BATCH = 2
SEQ = 8192
DEPTH = 1

MLA_HEADS = 32
MLA_Q_RANK = 1024
MLA_KV_RANK = 512
MLA_NOPE_DIM = 128
MLA_ROPE_DIM = 64
MLA_V_DIM = D_MODEL // MLA_HEADS
ROPE_THETA = 10000.0
SWA_HEADS = 64
SWA_KV_HEADS = 8
SWA_HEAD_DIM = D_MODEL // SWA_HEADS
SWA_GROUP = SWA_HEADS // SWA_KV_HEADS
WINDOW = 128
Q_BLOCK = 128
PEER_HEADS = 8
PEER_N_KEYS = 128
PEER_N_EXPERTS = PEER_N_KEYS * PEER_N_KEYS
PEER_QUERY_DIM = 256
PEER_HALF = PEER_QUERY_DIM // 2
PEER_TOPK = 16
PEER_CHUNK = 64

NORM_EPS = 1e-6
NEG_INF = -1e30
N_ADA = 6

W_MLA_Q = MLA_Q_RANK
W_MLA_KV = MLA_KV_RANK + MLA_ROPE_DIM
W_SWA_Q = SWA_HEADS * SWA_HEAD_DIM
W_SWA_KV = SWA_KV_HEADS * SWA_HEAD_DIM
OFF_MLA_KV = W_MLA_Q
OFF_SWA_Q = OFF_MLA_KV + W_MLA_KV
OFF_SWA_K = OFF_SWA_Q + W_SWA_Q
OFF_SWA_V = OFF_SWA_K + W_SWA_KV
OFF_GATE_A = OFF_SWA_V + W_SWA_KV
OFF_GATE_B = OFF_GATE_A + D_MODEL
IN_WIDTH = OFF_GATE_B + D_MODEL

kernel_name = "hybrid_mla_swa_sink_peer_block"


def rms_norm(x, g):
    x32 = x.astype(jnp.float32)
    y = x32 * lax.rsqrt(jnp.mean(x32 * x32, axis=-1, keepdims=True) + NORM_EPS)
    return (y * g.astype(jnp.float32)).astype(x.dtype)


def rotate_half_split(x, cos, sin):
    half = x.shape[-1] // 2
    x1, x2 = x[..., :half], x[..., half:]
    return jnp.concatenate([x1 * cos - x2 * sin, x2 * cos + x1 * sin], axis=-1)


def alibi_slopes(n):
    return jnp.exp2(-8.0 * jnp.arange(1, n + 1, dtype=jnp.float32) / n)


def mla_attention(zq, zkv, g_q_lat, w_q_b, g_kv_lat, w_kv_b):
    B, S, _ = zq.shape
    H, DN, DR, DV = MLA_HEADS, MLA_NOPE_DIM, MLA_ROPE_DIM, MLA_V_DIM
    q = (rms_norm(zq, g_q_lat) @ w_q_b).reshape(B, S, H, DN + DR)
    q_nope, q_pe = q[..., :DN], q[..., DN:]
    c_kv = rms_norm(zkv[..., :MLA_KV_RANK], g_kv_lat)
    k_pe = zkv[..., MLA_KV_RANK:]
    kv = (c_kv @ w_kv_b).reshape(B, S, H, DN + DV)
    k_nope, v = kv[..., :DN], kv[..., DN:]
    pos = jnp.arange(S, dtype=jnp.float32)
    inv_freq = ROPE_THETA ** (-jnp.arange(0, DR, 2, dtype=jnp.float32) / DR)
    ang = pos[:, None] * inv_freq[None, :]
    cos, sin = jnp.cos(ang).astype(zq.dtype), jnp.sin(ang).astype(zq.dtype)
    q_pe = rotate_half_split(q_pe, cos[None, :, None, :], sin[None, :, None, :])
    k_pe = rotate_half_split(k_pe, cos[None], sin[None])
    scale = (DN + DR) ** -0.5
    nb = S // Q_BLOCK
    qn_b = q_nope.reshape(B, nb, Q_BLOCK, H, DN).transpose(1, 0, 2, 3, 4)
    qp_b = q_pe.reshape(B, nb, Q_BLOCK, H, DR).transpose(1, 0, 2, 3, 4)
    kpos = jnp.arange(S)

    def block(args):
        qn, qp, i = args
        qpos = i * Q_BLOCK + jnp.arange(Q_BLOCK)
        s = (jnp.einsum('bqhd,bkhd->bhqk', qn, k_nope, preferred_element_type=jnp.float32)
             + jnp.einsum('bqhr,bkr->bhqk', qp, k_pe, preferred_element_type=jnp.float32)) * scale
        s = jnp.where(kpos[None, :] <= qpos[:, None], s, NEG_INF)
        p = jax.nn.softmax(s, axis=-1).astype(v.dtype)
        return jnp.einsum('bhqk,bkhd->bqhd', p, v)

    o = lax.map(block, (qn_b, qp_b, jnp.arange(nb)))
    return o.transpose(1, 0, 2, 3, 4).reshape(B, S, H * DV)


def swa_sink_attention(zq, zk, zv, sinks):
    B, S, _ = zq.shape
    NKV, G, HD = SWA_KV_HEADS, SWA_GROUP, SWA_HEAD_DIM
    nb = S // Q_BLOCK
    span = Q_BLOCK + WINDOW
    q = zq.reshape(B, nb, Q_BLOCK, NKV, G, HD).transpose(1, 0, 2, 3, 4, 5)
    pad = ((0, 0), (WINDOW, 0), (0, 0), (0, 0))
    k = jnp.pad(zk.reshape(B, S, NKV, HD), pad)
    v = jnp.pad(zv.reshape(B, S, NKV, HD), pad)
    slopes = alibi_slopes(SWA_HEADS).reshape(NKV, G)[None, :, :, None, None]
    sink = sinks.astype(jnp.float32).reshape(NKV, G)[None, :, :, None, None]
    scale = HD ** -0.5

    def block(args):
        qb, i = args
        start = i * Q_BLOCK
        kb = lax.dynamic_slice_in_dim(k, start, span, axis=1)
        vb = lax.dynamic_slice_in_dim(v, start, span, axis=1)
        qpos = start + jnp.arange(Q_BLOCK)
        kpos = start - WINDOW + jnp.arange(span)
        dist = qpos[:, None] - kpos[None, :]
        valid = (kpos[None, :] >= 0) & (dist >= 0) & (dist < WINDOW)
        s = (jnp.einsum('bqngd,bknd->bngqk', qb, kb, preferred_element_type=jnp.float32) * scale
             - slopes * dist.astype(jnp.float32))
        s = jnp.where(valid, s, NEG_INF)
        logits = jnp.concatenate([s, jnp.broadcast_to(sink, s.shape[:-1] + (1,))], axis=-1)
        p = jax.nn.softmax(logits, axis=-1)[..., :span].astype(vb.dtype)
        return jnp.einsum('bngqk,bknd->bqngd', p, vb)

    o = lax.map(block, (q, jnp.arange(nb)))
    return o.transpose(1, 0, 2, 3, 4, 5).reshape(B, S, SWA_HEADS * HD)


def peer_ffn(h, w_peer_q, sub_keys, u, v):
    B, S, D = h.shape
    T = B * S
    H, K = PEER_HEADS, PEER_TOPK
    xt = h.reshape(T, D)
    q = (xt @ w_peer_q).reshape(T, H, 2, PEER_HALF)
    s = jnp.einsum('thpd,hpnd->thpn', q, sub_keys, preferred_element_type=jnp.float32)
    top_s, top_i = lax.top_k(s, K)
    cand = top_s[:, :, 0, :, None] + top_s[:, :, 1, None, :]
    best_s, best_c = lax.top_k(cand.reshape(T, H, K * K), K)
    i1 = jnp.take_along_axis(top_i[:, :, 0], best_c // K, axis=-1)
    i2 = jnp.take_along_axis(top_i[:, :, 1], best_c % K, axis=-1)
    expert = i1 * PEER_N_KEYS + i2
    gate = jax.nn.softmax(best_s, axis=-1).astype(h.dtype)
    nc = T // PEER_CHUNK

    def chunk(args):
        xc, ec, gc = args
        u_sel = u[ec]
        v_sel = v[ec]
        a = jax.nn.gelu(jnp.einsum('cd,chkd->chk', xc, u_sel), approximate=False) * gc
        return jnp.einsum('chk,chkd->cd', a, v_sel)

    out = lax.map(chunk, (xt.reshape(nc, PEER_CHUNK, D),
                          expert.reshape(nc, PEER_CHUNK, H, K),
                          gate.reshape(nc, PEER_CHUNK, H, K)))
    return out.reshape(B, S, D)


def setup_inputs(seed: int = 0) -> dict:
    key = jax.random.key(seed)
    ks = jax.random.split(key, 20)
    D = D_MODEL
    f32 = jnp.float32

    def nrm(k, shape, std):
        return jax.random.normal(k, shape, f32) * std

    def gain(k, n):
        return 1.0 + 0.1 * jax.random.normal(k, (DEPTH, n), f32)

    return {
        "x": jax.random.normal(ks[0], (BATCH, SEQ, D), f32),
        "c": jax.random.normal(ks[1], (BATCH, D), f32),
        "w_ada": nrm(ks[2], (DEPTH, D, N_ADA * D), 0.5 * D ** -0.5),
        "b_ada": nrm(ks[3], (DEPTH, N_ADA * D), 0.02),
        "g_attn_pre": gain(ks[4], D),
        "g_attn_post": gain(ks[5], D),
        "w_in": nrm(ks[6], (DEPTH, D, IN_WIDTH), D ** -0.5),
        "g_q_lat": gain(ks[7], MLA_Q_RANK),
        "w_q_b": nrm(ks[8], (DEPTH, MLA_Q_RANK, MLA_HEADS * (MLA_NOPE_DIM + MLA_ROPE_DIM)), MLA_Q_RANK ** -0.5),
        "g_kv_lat": gain(ks[9], MLA_KV_RANK),
        "w_kv_b": nrm(ks[10], (DEPTH, MLA_KV_RANK, MLA_HEADS * (MLA_NOPE_DIM + MLA_V_DIM)), MLA_KV_RANK ** -0.5),
        "sinks": nrm(ks[11], (DEPTH, SWA_HEADS), 0.5),
        "w_out": nrm(ks[12], (DEPTH, D, D), D ** -0.5),
        "g_ffn_pre": gain(ks[13], D),
        "g_ffn_post": gain(ks[14], D),
        "w_peer_q": nrm(ks[15], (DEPTH, D, PEER_HEADS * PEER_QUERY_DIM), D ** -0.5),
        "peer_sub_keys": nrm(ks[16], (DEPTH, PEER_HEADS, 2, PEER_N_KEYS, PEER_HALF), PEER_HALF ** -0.5),
        "peer_u": nrm(ks[17], (DEPTH, PEER_N_EXPERTS, D), D ** -0.5),
        "peer_v": nrm(ks[18], (DEPTH, PEER_N_EXPERTS, D), (PEER_HEADS * PEER_TOPK) ** -0.5),
    }


def reference(x, c, w_ada, b_ada, g_attn_pre, g_attn_post, w_in, g_q_lat, w_q_b, g_kv_lat,
              w_kv_b, sinks, w_out, g_ffn_pre, g_ffn_post, w_peer_q, peer_sub_keys, peer_u, peer_v):
    B, S, D = x.shape
    cond = jax.nn.silu(c)
    for l in range(DEPTH):
        ada = (cond @ w_ada[l] + b_ada[l]).reshape(B, N_ADA, 1, D)
        shift_a, scale_a, gate_a = ada[:, 0], ada[:, 1], ada[:, 2]
        shift_f, scale_f, gate_f = ada[:, 3], ada[:, 4], ada[:, 5]

        h = rms_norm(x, g_attn_pre[l]) * (1 + scale_a) + shift_a
        z = h @ w_in[l]
        zq, zkv, zsq, zsk, zsv, za, zb = jnp.split(
            z, [OFF_MLA_KV, OFF_SWA_Q, OFF_SWA_K, OFF_SWA_V, OFF_GATE_A, OFF_GATE_B], axis=-1)
        o_a = mla_attention(zq, zkv, g_q_lat[l], w_q_b[l], g_kv_lat[l], w_kv_b[l])
        o_b = swa_sink_attention(zsq, zsk, zsv, sinks[l])
        mix = jax.nn.sigmoid(za) * o_a + jax.nn.sigmoid(zb) * o_b
        x = x + gate_a * rms_norm(mix @ w_out[l], g_attn_post[l])

        h = rms_norm(x, g_ffn_pre[l]) * (1 + scale_f) + shift_f
        y = peer_ffn(h, w_peer_q[l], peer_sub_keys[l], peer_u[l], peer_v[l])
        x = x + gate_f * rms_norm(y, g_ffn_post[l])
    return x
```

```python
import functools
import math

import jax
import jax.numpy as jnp
from jax import lax
from jax.experimental import pallas as pl
from jax.experimental.pallas import tpu as pltpu

F32 = jnp.float32
BF16 = jnp.bfloat16

MLA_NOPE_DIM = 128
MLA_ROPE_DIM = 64
MLA_V_DIM = 128
MLA_QK_PAD = 256
ROPE_THETA = 10000.0
SWA_HEAD_DIM = 64
SWA_KV_HEADS = 8
WINDOW = 128
PEER_HEADS = 8
PEER_N_KEYS = 128
PEER_TOPK = 16
N_ADA = 6
NORM_EPS = 1e-6
NEG_INF = -1e30

LANES = 128
VMEM_LIMIT = 56 * 1024 * 1024


def _params(sem, vmem=VMEM_LIMIT):
    return pltpu.CompilerParams(dimension_semantics=sem, vmem_limit_bytes=vmem)


def _rms(x, g):
    return x * lax.rsqrt(jnp.mean(x * x, axis=-1, keepdims=True) + NORM_EPS) * g


def _ada_kernel(cb_ref, w_ref, b_ref, o_ref):
    nb = cb_ref.shape[0]
    tn = w_ref.shape[1]
    for b in range(nb):
        cb = cb_ref[b]
        cond = cb * jax.nn.sigmoid(cb)
        for j in range(tn // LANES):
            sl = slice(j * LANES, (j + 1) * LANES)
            r = jnp.sum(w_ref[:, sl] * cond, axis=0, keepdims=True)
            o_ref[b:b + 1, sl] = r + b_ref[:, sl]


def _ada(c, w, bias):
    nb, k = c.shape
    n = w.shape[1]
    tn = min(512, n)
    cb = jnp.broadcast_to(c[:, :, None], (nb, k, LANES))
    return pl.pallas_call(
        _ada_kernel,
        grid=(n // tn,),
        in_specs=[pl.BlockSpec((nb, k, LANES), lambda j: (0, 0, 0)),
                  pl.BlockSpec((k, tn), lambda j: (0, j)),
                  pl.BlockSpec((1, tn), lambda j: (0, j))],
        out_specs=pl.BlockSpec((nb, tn), lambda j: (0, j)),
        out_shape=jax.ShapeDtypeStruct((nb, n), F32),
        compiler_params=_params(("arbitrary",)),
        name="ada_matvec",
    )(cb, w, bias.reshape(1, n))


def _prenorm_kernel(x_ref, g_ref, sc_ref, sh_ref, o_ref):
    h = _rms(x_ref[...], g_ref[...]) * (1.0 + sc_ref[0]) + sh_ref[0]
    o_ref[...] = h.astype(o_ref.dtype)


def _prenorm(x2, g, scale, shift, seq):
    t, d = x2.shape
    tm = min(256, seq)
    per = seq // tm
    row = pl.BlockSpec((tm, d), lambda i: (i, 0))
    mod = pl.BlockSpec((1, 1, d), lambda i: (i // per, 0, 0))
    return pl.pallas_call(
        _prenorm_kernel,
        grid=(t // tm,),
        in_specs=[row, pl.BlockSpec((1, d), lambda i: (0, 0)), mod, mod],
        out_specs=row,
        out_shape=jax.ShapeDtypeStruct((t, d), BF16),
        compiler_params=_params(("arbitrary",)),
        name="prenorm_modulate",
    )(x2, g.reshape(1, d), scale, shift)


def _post_kernel(x_ref, y_ref, gpost_ref, gate_ref, gpre_ref, sc_ref, sh_ref, x1_ref, h_ref):
    x1 = x_ref[...] + gate_ref[0] * _rms(y_ref[...], gpost_ref[...])
    x1_ref[...] = x1
    h = _rms(x1, gpre_ref[...]) * (1.0 + sc_ref[0]) + sh_ref[0]
    h_ref[...] = h.astype(h_ref.dtype)


def _post_attn(x2, y, g_post, gate, g_pre, scale, shift, seq):
    t, d = x2.shape
    tm = min(256, seq)
    per = seq // tm
    row = pl.BlockSpec((tm, d), lambda i: (i, 0))
    vec = pl.BlockSpec((1, d), lambda i: (0, 0))
    mod = pl.BlockSpec((1, 1, d), lambda i: (i // per, 0, 0))
    return pl.pallas_call(
        _post_kernel,
        grid=(t // tm,),
        in_specs=[row, row, vec, mod, vec, mod, mod],
        out_specs=[row, row],
        out_shape=[jax.ShapeDtypeStruct((t, d), F32), jax.ShapeDtypeStruct((t, d), BF16)],
        compiler_params=_params(("arbitrary",)),
        name="post_attn_norm",
    )(x2, y, g_post.reshape(1, d), gate, g_pre.reshape(1, d), scale, shift)


def _final_kernel(x_ref, y_ref, g_ref, gate_ref, o_ref):
    o_ref[...] = x_ref[...] + gate_ref[0] * _rms(y_ref[...], g_ref[...])


def _final(x1, y, g, gate, seq):
    t, d = x1.shape
    tm = min(256, seq)
    per = seq // tm
    row = pl.BlockSpec((tm, d), lambda i: (i, 0))
    return pl.pallas_call(
        _final_kernel,
        grid=(t // tm,),
        in_specs=[row, row, pl.BlockSpec((1, d), lambda i: (0, 0)),
                  pl.BlockSpec((1, 1, d), lambda i: (i // per, 0, 0))],
        out_specs=row,
        out_shape=jax.ShapeDtypeStruct((t, d), F32),
        compiler_params=_params(("arbitrary",)),
        name="final_residual",
    )(x1, y, g.reshape(1, d), gate)


def _mm_kernel(a_ref, w_ref, o_ref):
    o_ref[...] = jnp.dot(a_ref[...], w_ref[...], preferred_element_type=F32).astype(o_ref.dtype)


def _matmul(a, w, out_dtype, tn, tm=1024, name="matmul"):
    m, k = a.shape
    n = w.shape[1]
    tm = min(tm, m)
    tn = min(tn, n)
    return pl.pallas_call(
        _mm_kernel,
        grid=(m // tm, n // tn),
        in_specs=[pl.BlockSpec((tm, k), lambda i, j: (i, 0)),
                  pl.BlockSpec((k, tn), lambda i, j: (0, j))],
        out_specs=pl.BlockSpec((tm, tn), lambda i, j: (i, j)),
        out_shape=jax.ShapeDtypeStruct((m, n), out_dtype),
        compiler_params=_params(("arbitrary", "arbitrary")),
        name=name,
    )(a, w)


def _rope_upper(hi, cos_ref, sin_hi_ref, sin_lo_ref):
    return (hi * cos_ref[...]
            + pltpu.roll(hi, 32, axis=1) * sin_hi_ref[...]
            + pltpu.roll(hi, 96, axis=1) * sin_lo_ref[...])


def _mla_q_kernel(z_ref, g_ref, w_ref, cos_ref, s1_ref, s2_ref, o_ref, *, scale, heads):
    qn = _rms(z_ref[...], g_ref[...]).astype(BF16)
    y = jnp.dot(qn, w_ref[...], preferred_element_type=F32)
    for h in range(heads):
        lo = y[:, h * MLA_QK_PAD:h * MLA_QK_PAD + LANES]
        hi = y[:, h * MLA_QK_PAD + LANES:(h + 1) * MLA_QK_PAD]
        o_ref[:, h * MLA_QK_PAD:h * MLA_QK_PAD + LANES] = (lo * scale).astype(o_ref.dtype)
        o_ref[:, h * MLA_QK_PAD + LANES:(h + 1) * MLA_QK_PAD] = (
            _rope_upper(hi, cos_ref, s1_ref, s2_ref) * scale).astype(o_ref.dtype)


def _mla_q(z_a, g_q, w_q, rope, seq, q_rank, n_heads):
    t = z_a.shape[0]
    tm = min(512, seq)
    hb = min(4, n_heads)
    per = seq // tm
    scale = (MLA_NOPE_DIM + MLA_ROPE_DIM) ** -0.5
    tab = pl.BlockSpec((tm, LANES), lambda i, j: (i % per, 0))
    return pl.pallas_call(
        functools.partial(_mla_q_kernel, scale=scale, heads=hb),
        grid=(t // tm, n_heads // hb),
        in_specs=[pl.BlockSpec((tm, q_rank), lambda i, j: (i, 0)),
                  pl.BlockSpec((1, q_rank), lambda i, j: (0, 0)),
                  pl.BlockSpec((q_rank, hb * MLA_QK_PAD), lambda i, j: (0, j)),
                  tab, tab, tab],
        out_specs=pl.BlockSpec((tm, hb * MLA_QK_PAD), lambda i, j: (i, j)),
        out_shape=jax.ShapeDtypeStruct((t, n_heads * MLA_QK_PAD), BF16),
        compiler_params=_params(("arbitrary", "arbitrary")),
        name="mla_q_proj",
    )(z_a, g_q.reshape(1, q_rank), w_q, *rope)


def _mla_kv_kernel(c_ref, pe_ref, g_ref, wk_ref, wv_ref, cos_ref, s1_ref, s2_ref,
                   k_ref, v_ref, *, heads):
    c = _rms(c_ref[...], g_ref[...]).astype(BF16)
    kn = jnp.dot(c, wk_ref[...], preferred_element_type=F32)
    v_ref[...] = jnp.dot(c, wv_ref[...], preferred_element_type=F32).astype(v_ref.dtype)
    pe = _rope_upper(pe_ref[...], cos_ref, s1_ref, s2_ref).astype(k_ref.dtype)
    for h in range(heads):
        k_ref[:, h * MLA_QK_PAD:h * MLA_QK_PAD + LANES] = (
            kn[:, h * LANES:(h + 1) * LANES].astype(k_ref.dtype))
        k_ref[:, h * MLA_QK_PAD + LANES:(h + 1) * MLA_QK_PAD] = pe


def _mla_kv(z_a, g_kv, w_kn, w_v, rope, seq, q_rank, kv_rank, n_heads):
    t = z_a.shape[0]
    tm = min(256, seq)
    per = seq // tm
    tab = pl.BlockSpec((tm, LANES), lambda i: (i % per, 0))
    full = lambda shape: pl.BlockSpec(shape, lambda i: (0, 0))
    return pl.pallas_call(
        functools.partial(_mla_kv_kernel, heads=n_heads),
        grid=(t // tm,),
        in_specs=[pl.BlockSpec((tm, kv_rank), lambda i: (i, q_rank // kv_rank)),
                  pl.BlockSpec((tm, LANES), lambda i: (i, (q_rank + kv_rank) // LANES)),
                  full((1, kv_rank)),
                  full((kv_rank, n_heads * MLA_NOPE_DIM)),
                  full((kv_rank, n_heads * MLA_V_DIM)),
                  tab, tab, tab],
        out_specs=[pl.BlockSpec((tm, n_heads * MLA_QK_PAD), lambda i: (i, 0)),
                   pl.BlockSpec((tm, n_heads * MLA_V_DIM), lambda i: (i, 0))],
        out_shape=[jax.ShapeDtypeStruct((t, n_heads * MLA_QK_PAD), BF16),
                   jax.ShapeDtypeStruct((t, n_heads * MLA_V_DIM), BF16)],
        compiler_params=_params(("arbitrary",)),
        name="mla_kv_proj",
    )(z_a, z_a, g_kv.reshape(1, kv_rank), w_kn, w_v, *rope)


def _mla_attn_kernel(q_ref, k_ref, v_ref, za_ref, o_ref, *, tq):
    qi = pl.program_id(2)
    q = q_ref[...]

    def scores(j):
        kb = k_ref[pl.ds(pl.multiple_of(j * tq, tq), tq), :]
        return lax.dot_general(q, kb, (((1,), (1,)), ((), ())), preferred_element_type=F32)

    def update(carry, s, j):
        m, l, acc = carry
        m_new = jnp.maximum(m, jnp.max(s, axis=-1, keepdims=True))
        alpha = jnp.exp(m - m_new)
        p = jnp.exp(s - m_new)
        vb = v_ref[pl.ds(pl.multiple_of(j * tq, tq), tq), :]
        acc = alpha * acc + jnp.dot(p.astype(BF16), vb, preferred_element_type=F32)
        l = alpha * l + jnp.sum(p, axis=-1, keepdims=True)
        return m_new, l, acc

    def body(j, carry):
        return update(carry, scores(j), j)

    init = (jnp.full((tq, 1), NEG_INF, F32), jnp.zeros((tq, 1), F32),
            jnp.zeros((tq, MLA_V_DIM), F32))
    carry = lax.fori_loop(0, qi, body, init)
    row = lax.broadcasted_iota(jnp.int32, (tq, tq), 0)
    col = lax.broadcasted_iota(jnp.int32, (tq, tq), 1)
    s = jnp.where(col <= row, scores(qi), NEG_INF)
    m, l, acc = update(carry, s, qi)
    o_ref[...] = (acc / l * jax.nn.sigmoid(za_ref[...])).astype(o_ref.dtype)


def _mla_attn(q, k, v, z_g, batch, seq, n_heads):
    t = q.shape[0]
    tq = min(512, seq)
    nq = seq // tq
    return pl.pallas_call(
        functools.partial(_mla_attn_kernel, tq=tq),
        grid=(batch, n_heads, nq),
        in_specs=[pl.BlockSpec((tq, MLA_QK_PAD), lambda b, h, i: (b * nq + i, h)),
                  pl.BlockSpec((seq, MLA_QK_PAD), lambda b, h, i: (b, h)),
                  pl.BlockSpec((seq, MLA_V_DIM), lambda b, h, i: (b, h)),
                  pl.BlockSpec((tq, MLA_V_DIM), lambda b, h, i: (b * nq + i, h))],
        out_specs=pl.BlockSpec((tq, MLA_V_DIM), lambda b, h, i: (b * nq + i, h)),
        out_shape=jax.ShapeDtypeStruct((t, n_heads * MLA_V_DIM), BF16),
        compiler_params=_params(("arbitrary", "arbitrary", "arbitrary")),
        name="mla_flash_attention",
    )(q, k, v, z_g)


def _swa_kernel(sink_ref, q_ref, kp_ref, kc_ref, vp_ref, vc_ref, zb_ref, a_ref, o_ref,
                *, per, n_heads):
    i = pl.program_id(0)
    has_prev = (i % per) > 0
    group = n_heads // SWA_KV_HEADS
    hd = SWA_HEAD_DIM
    scale = hd ** -0.5
    span = 2 * WINDOW
    row = lax.broadcasted_iota(jnp.int32, (WINDOW, span), 0)
    col = lax.broadcasted_iota(jnp.int32, (WINDOW, span), 1)
    dist = row + WINDOW - col
    valid = (dist >= 0) & (dist < WINDOW) & ((col >= WINDOW) | has_prev)
    distf = dist.astype(F32)
    for n in range(SWA_KV_HEADS):
        ksl = slice(n * hd, (n + 1) * hd)
        kb = jnp.concatenate([kp_ref[:, ksl], kc_ref[:, ksl]], axis=0).astype(BF16)
        vb = jnp.concatenate([vp_ref[:, ksl], vc_ref[:, ksl]], axis=0).astype(BF16)
        for g in range(group):
            hq = n * group + g
            slope = 2.0 ** (-8.0 * (hq + 1) / n_heads)
            qsl = slice(hq * hd, (hq + 1) * hd)
            qh = q_ref[:, qsl].astype(BF16)
            s = lax.dot_general(qh, kb, (((1,), (1,)), ((), ())), preferred_element_type=F32)
            s = jnp.where(valid, s * scale - slope * distf, NEG_INF)
            sink = sink_ref[hq]
            m = jnp.maximum(jnp.max(s, axis=-1, keepdims=True), sink)
            p = jnp.exp(s - m)
            den = jnp.sum(p, axis=-1, keepdims=True) + jnp.exp(sink - m)
            o = jnp.dot((p / den).astype(BF16), vb, preferred_element_type=F32)
            mix = a_ref[:, qsl].astype(F32) + jax.nn.sigmoid(zb_ref[:, qsl]) * o
            o_ref[:, qsl] = mix.astype(o_ref.dtype)


def _swa(z_swa, z_g, gated_a, sinks, seq, d):
    t = z_swa.shape[0]
    n_heads = d // SWA_HEAD_DIM
    kvw = SWA_KV_HEADS * SWA_HEAD_DIM
    per = seq // WINDOW
    kcol = d // kvw
    prev = lambda c: (lambda i: (jnp.maximum(i - 1, 0), c))
    cur = lambda c: (lambda i: (i, c))
    return pl.pallas_call(
        functools.partial(_swa_kernel, per=per, n_heads=n_heads),
        grid=(t // WINDOW,),
        in_specs=[pl.BlockSpec(memory_space=pltpu.SMEM),
                  pl.BlockSpec((WINDOW, d), cur(0)),
                  pl.BlockSpec((WINDOW, kvw), prev(kcol)),
                  pl.BlockSpec((WINDOW, kvw), cur(kcol)),
                  pl.BlockSpec((WINDOW, kvw), prev(kcol + 1)),
                  pl.BlockSpec((WINDOW, kvw), cur(kcol + 1)),
                  pl.BlockSpec((WINDOW, d), cur(1)),
                  pl.BlockSpec((WINDOW, d), cur(0))],
        out_specs=pl.BlockSpec((WINDOW, d), cur(0)),
        out_shape=jax.ShapeDtypeStruct((t, d), BF16),
        compiler_params=_params(("arbitrary",)),
        name="swa_sink_attention",
    )(sinks, z_swa, z_swa, z_swa, z_swa, z_swa, z_g, gated_a)


def _top16(s):
    rows = lax.broadcasted_iota(jnp.int32, s.shape, 0).astype(F32)
    rank = jnp.full(s.shape, float(PEER_TOPK), F32)
    vals = []
    for r in range(PEER_TOPK):
        m = jnp.max(s, axis=0, keepdims=True)
        first = jnp.min(jnp.where(s == m, rows, float(PEER_N_KEYS)), axis=0, keepdims=True)
        hit = rows == first
        rank = jnp.where(hit, float(r), rank)
        s = jnp.where(hit, NEG_INF, s)
        vals.append(m)
    return vals, rank


def _peer_select(s1, s2):
    k = PEER_TOPK
    v1, rank1 = _top16(s1)
    v2, rank2 = _top16(s2)
    n = s1.shape[1]
    r16 = lax.broadcasted_iota(jnp.int32, (k, n), 0).astype(F32)
    v2m = jnp.zeros((k, n), F32)
    for b in range(k):
        v2m = jnp.where(r16 == float(b), v2[b], v2m)
    cand = [v1[a] + v2m for a in range(k)]
    code = [r16 + float(a * k) for a in range(k)]
    cand = [jnp.where((r16 + 1.0) * float(a + 1) <= float(k), cand[a], NEG_INF) for a in range(k)]
    sel = [jnp.zeros((k, n), F32) for _ in range(k)]
    for _ in range(k):
        m = functools.reduce(jnp.maximum, cand)
        m = jnp.max(m, axis=0, keepdims=True)
        first = functools.reduce(
            jnp.minimum, [jnp.where(cand[a] == m, code[a], float(k * k)) for a in range(k)])
        first = jnp.min(first, axis=0, keepdims=True)
        for a in range(k):
            hit = code[a] == first
            sel[a] = jnp.where(hit, 1.0, sel[a])
            cand[a] = jnp.where(hit, NEG_INF, cand[a])
    e2m = jnp.exp(v2m - v2[0])
    z = jnp.zeros((1, n), F32)
    n1 = jnp.zeros(s1.shape, F32)
    for a in range(k):
        cnt = jnp.sum(sel[a], axis=0, keepdims=True)
        z = z + jnp.exp(v1[a] - v1[0]) * jnp.sum(sel[a] * e2m, axis=0, keepdims=True)
        n1 = jnp.where(rank1 == float(a), cnt, n1)
    e1 = jnp.exp(s1 - v1[0])
    e2 = jnp.exp(s2 - v2[0]) / z
    return n1, e1, rank2, e2


def _peer_topk_kernel(q_ref, key_ref, n1_ref, e1_ref, b2_ref, e2_ref, *, sub):
    tm = q_ref.shape[0]
    half = q_ref.shape[1] // 2
    dn = (((1,), (1,)), ((), ()))
    for c in range(tm // sub):
        cs = slice(c * sub, (c + 1) * sub)
        q1 = q_ref[cs, :half]
        q2 = q_ref[cs, half:]
        s1 = lax.dot_general(key_ref[0, 0], q1, dn, preferred_element_type=F32)
        s2 = lax.dot_general(key_ref[0, 1], q2, dn, preferred_element_type=F32)
        n1, e1, b2, e2 = _peer_select(s1, s2)
        n1_ref[:, cs] = n1
        e1_ref[:, cs] = e1
        b2_ref[:, cs] = b2.astype(b2_ref.dtype)
        e2_ref[:, cs] = e2.astype(e2_ref.dtype)


def _peer_topk(q, keys):
    t = q.shape[0]
    heads = keys.shape[0]
    qd = q.shape[1] // heads
    tm = min(256, t)
    out = pl.BlockSpec((PEER_N_KEYS, tm), lambda i, h: (h, i))
    shape = lambda dt: jax.ShapeDtypeStruct((heads * PEER_N_KEYS, t), dt)
    return pl.pallas_call(
        functools.partial(_peer_topk_kernel, sub=min(128, tm)),
        grid=(t // tm, heads),
        in_specs=[pl.BlockSpec((tm, qd), lambda i, h: (i, h)),
                  pl.BlockSpec((1, 2, PEER_N_KEYS, qd // 2), lambda i, h: (h, 0, 0, 0))],
        out_specs=[out, out, out, out],
        out_shape=[shape(F32), shape(F32), shape(BF16), shape(BF16)],
        compiler_params=_params(("arbitrary", "arbitrary")),
        name="peer_topk",
    )(q, keys)


def _peer_dense_kernel(ht_ref, u_ref, vt_ref, n1_ref, e1_ref, b2_ref, e2_ref, o_ref, *, heads):
    c = pl.program_id(1)
    tn = u_ref.shape[0]
    nk = PEER_N_KEYS
    hid = jnp.dot(u_ref[...], ht_ref[...], preferred_element_type=F32)
    act = 0.5 * hid * (1.0 + lax.erf(hid * (2.0 ** -0.5)))
    parts = []
    for k in range(tn // nk):
        i1 = c * (tn // nk) + k
        gate = jnp.zeros((nk, hid.shape[1]), F32)
        for h in range(heads):
            n1 = n1_ref[pl.ds(h * nk + i1, 1), :]
            e1 = e1_ref[pl.ds(h * nk + i1, 1), :]
            b2 = b2_ref[h * nk:(h + 1) * nk, :].astype(F32)
            e2 = e2_ref[h * nk:(h + 1) * nk, :].astype(F32)
            gate = gate + jnp.where(b2 < n1, e2, 0.0) * e1
        parts.append((act[k * nk:(k + 1) * nk, :] * gate).astype(BF16))
    a = jnp.concatenate(parts, axis=0) if len(parts) > 1 else parts[0]
    y = jnp.dot(vt_ref[...], a, preferred_element_type=F32)

    @pl.when(c == 0)
    def _():
        o_ref[...] = y

    @pl.when(c > 0)
    def _():
        o_ref[...] += y


def _peer_dense(ht, u, vt, n1, e1, b2, e2, heads):
    d, t = ht.shape
    n_exp = u.shape[0]
    tm = min(512, t)
    tn = min(256, n_exp)
    rows = heads * PEER_N_KEYS
    sel = pl.BlockSpec((rows, tm), lambda i, c: (0, i))
    return pl.pallas_call(
        functools.partial(_peer_dense_kernel, heads=heads),
        grid=(t // tm, n_exp // tn),
        in_specs=[pl.BlockSpec((d, tm), lambda i, c: (0, i)),
                  pl.BlockSpec((tn, d), lambda i, c: (c, 0)),
                  pl.BlockSpec((d, tn), lambda i, c: (0, c)),
                  sel, sel, sel, sel],
        out_specs=pl.BlockSpec((d, tm), lambda i, c: (0, i)),
        out_shape=jax.ShapeDtypeStruct((d, t), F32),
        compiler_params=_params(("arbitrary", "arbitrary")),
        name="peer_dense_experts",
    )(ht, u, vt, n1, e1, b2, e2)


def _rope_tables(seq):
    half = MLA_ROPE_DIM // 2
    pos = jnp.arange(seq, dtype=F32)
    inv_freq = ROPE_THETA ** (-jnp.arange(0, MLA_ROPE_DIM, 2, dtype=F32) / MLA_ROPE_DIM)
    ang = pos[:, None] * inv_freq[None, :]
    cos, sin = jnp.cos(ang), jnp.sin(ang)
    zero = jnp.zeros((seq, half), F32)
    pad = jnp.zeros((seq, LANES - MLA_ROPE_DIM), F32)
    cos_t = jnp.concatenate([cos, cos, pad], axis=1)
    sin_hi = jnp.concatenate([zero, sin, pad], axis=1)
    sin_lo = jnp.concatenate([-sin, zero, pad], axis=1)
    return cos_t, sin_hi, sin_lo


def _layer(x2, cond_in, batch, seq, w_ada, b_ada, g_attn_pre, g_attn_post, w_in, g_q_lat, w_q_b,
           g_kv_lat, w_kv_b, sinks, w_out, g_ffn_pre, g_ffn_post, w_peer_q, sub_keys, peer_u, peer_v):
    t, d = x2.shape
    q_rank = g_q_lat.shape[0]
    kv_rank = g_kv_lat.shape[0]
    n_mla = d // MLA_V_DIM
    kvw = SWA_KV_HEADS * SWA_HEAD_DIM
    heads = sub_keys.shape[0]

    ada = _ada(cond_in, w_ada, b_ada).reshape(batch, N_ADA, 1, d)
    shift_a, scale_a, gate_a, shift_f, scale_f, gate_f = [ada[:, i] for i in range(N_ADA)]

    w_a_cols = q_rank + kv_rank + MLA_ROPE_DIM
    w_a = jnp.pad(w_in[:, :w_a_cols], ((0, 0), (0, -w_a_cols % (2 * LANES)))).astype(BF16)
    off_swa = w_a_cols
    off_gate = off_swa + d + 2 * kvw
    w_swa = w_in[:, off_swa:off_gate].astype(BF16)
    w_g = w_in[:, off_gate:].astype(BF16)
    qk = MLA_NOPE_DIM + MLA_ROPE_DIM
    w_q = jnp.pad(w_q_b.reshape(q_rank, n_mla, qk),
                  ((0, 0), (0, 0), (0, MLA_QK_PAD - qk))).reshape(q_rank, n_mla * MLA_QK_PAD).astype(BF16)
    w_kv = w_kv_b.reshape(kv_rank, n_mla, MLA_NOPE_DIM + MLA_V_DIM)
    w_kn = w_kv[:, :, :MLA_NOPE_DIM].reshape(kv_rank, n_mla * MLA_NOPE_DIM).astype(BF16)
    w_v = w_kv[:, :, MLA_NOPE_DIM:].reshape(kv_rank, n_mla * MLA_V_DIM).astype(BF16)
    rope = _rope_tables(seq)

    h = _prenorm(x2, g_attn_pre, scale_a, shift_a, seq)
    z_a = _matmul(h, w_a, F32, tn=w_a.shape[1] // 2, name="in_proj_mla")
    z_swa = _matmul(h, w_swa, F32, tn=512, name="in_proj_swa")
    z_g = _matmul(h, w_g, F32, tn=512, name="in_proj_gates")
    q = _mla_q(z_a, g_q_lat, w_q, rope, seq, q_rank, n_mla)
    k, v = _mla_kv(z_a, g_kv_lat, w_kn, w_v, rope, seq, q_rank, kv_rank, n_mla)
    gated_a = _mla_attn(q, k, v, z_g, batch, seq, n_mla)
    mix = _swa(z_swa, z_g, gated_a, sinks, seq, d)
    y = _matmul(mix, w_out.astype(BF16), F32, tn=512, name="out_proj")
    x1, h2 = _post_attn(x2, y, g_attn_post, gate_a, g_ffn_pre, scale_f, shift_f, seq)

    pq = _matmul(h2, w_peer_q.astype(BF16), BF16, tn=512, name="peer_query")
    n1, e1, b2, e2 = _peer_topk(pq, sub_keys.astype(BF16))
    yt = _peer_dense(h2.T, peer_u.astype(BF16), peer_v.T.astype(BF16), n1, e1, b2, e2, heads)
    return _final(x1, yt.T, g_ffn_post, gate_f, seq)


def kernel(x, c, w_ada, b_ada, g_attn_pre, g_attn_post, w_in, g_q_lat, w_q_b, g_kv_lat, w_kv_b, sinks, w_out, g_ffn_pre, g_ffn_post, w_peer_q, peer_sub_keys, peer_u, peer_v):
    batch, seq, d = x.shape
    x2 = x.reshape(batch * seq, d)
    for l in range(w_ada.shape[0]):
        x2 = _layer(x2, c, batch, seq, w_ada[l], b_ada[l], g_attn_pre[l], g_attn_post[l], w_in[l],
                    g_q_lat[l], w_q_b[l], g_kv_lat[l], w_kv_b[l], sinks[l], w_out[l], g_ffn_pre[l],
                    g_ffn_post[l], w_peer_q[l], peer_sub_keys[l], peer_u[l], peer_v[l])
    return x2.reshape(batch, seq, d)
```

```python
import functools
import math

import jax
import jax.numpy as jnp
from jax import lax
from jax.experimental import pallas as pl
from jax.experimental.pallas import tpu as pltpu

F32 = jnp.float32
BF16 = jnp.bfloat16

MLA_NOPE_DIM = 128
MLA_ROPE_DIM = 64
MLA_V_DIM = 128
MLA_QK_PAD = 256
ROPE_THETA = 10000.0
SWA_HEAD_DIM = 64
SWA_KV_HEADS = 8
WINDOW = 128
PEER_HEADS = 8
PEER_N_KEYS = 128
PEER_TOPK = 16
N_ADA = 6
NORM_EPS = 1e-6
NEG_INF = -1e30

LANES = 128
VMEM_LIMIT = 56 * 1024 * 1024


def _params(sem, vmem=VMEM_LIMIT):
    return pltpu.CompilerParams(dimension_semantics=sem, vmem_limit_bytes=vmem)


def _rms(x, g):
    return x * lax.rsqrt(jnp.mean(x * x, axis=-1, keepdims=True) + NORM_EPS) * g


def _ada_kernel(cb_ref, w_ref, b_ref, o_ref):
    nb = cb_ref.shape[0]
    tn = w_ref.shape[1]
    for b in range(nb):
        cb = cb_ref[b]
        cond = cb * jax.nn.sigmoid(cb)
        for j in range(tn // LANES):
            sl = slice(j * LANES, (j + 1) * LANES)
            r = jnp.sum(w_ref[:, sl] * cond, axis=0, keepdims=True)
            o_ref[b:b + 1, sl] = r + b_ref[:, sl]


def _ada(c, w, bias):
    nb, k = c.shape
    n = w.shape[1]
    tn = min(512, n)
    cb = jnp.broadcast_to(c[:, :, None], (nb, k, LANES))
    return pl.pallas_call(
        _ada_kernel,
        grid=(n // tn,),
        in_specs=[pl.BlockSpec((nb, k, LANES), lambda j: (0, 0, 0)),
                  pl.BlockSpec((k, tn), lambda j: (0, j)),
                  pl.BlockSpec((1, tn), lambda j: (0, j))],
        out_specs=pl.BlockSpec((nb, tn), lambda j: (0, j)),
        out_shape=jax.ShapeDtypeStruct((nb, n), F32),
        compiler_params=_params(("arbitrary",)),
        name="ada_matvec",
    )(cb, w, bias.reshape(1, n))


def _prenorm_kernel(x_ref, g_ref, sc_ref, sh_ref, o_ref):
    h = _rms(x_ref[...], g_ref[...]) * (1.0 + sc_ref[0]) + sh_ref[0]
    o_ref[...] = h.astype(o_ref.dtype)


def _prenorm(x2, g, scale, shift, seq):
    t, d = x2.shape
    tm = min(256, seq)
    per = seq // tm
    row = pl.BlockSpec((tm, d), lambda i: (i, 0))
    mod = pl.BlockSpec((1, 1, d), lambda i: (i // per, 0, 0))
    return pl.pallas_call(
        _prenorm_kernel,
        grid=(t // tm,),
        in_specs=[row, pl.BlockSpec((1, d), lambda i: (0, 0)), mod, mod],
        out_specs=row,
        out_shape=jax.ShapeDtypeStruct((t, d), BF16),
        compiler_params=_params(("arbitrary",)),
        name="prenorm_modulate",
    )(x2, g.reshape(1, d), scale, shift)


def _post_kernel(x_ref, y_ref, gpost_ref, gate_ref, gpre_ref, sc_ref, sh_ref, x1_ref, h_ref):
    x1 = x_ref[...] + gate_ref[0] * _rms(y_ref[...], gpost_ref[...])
    x1_ref[...] = x1
    h = _rms(x1, gpre_ref[...]) * (1.0 + sc_ref[0]) + sh_ref[0]
    h_ref[...] = h.astype(h_ref.dtype)


def _post_attn(x2, y, g_post, gate, g_pre, scale, shift, seq):
    t, d = x2.shape
    tm = min(256, seq)
    per = seq // tm
    row = pl.BlockSpec((tm, d), lambda i: (i, 0))
    vec = pl.BlockSpec((1, d), lambda i: (0, 0))
    mod = pl.BlockSpec((1, 1, d), lambda i: (i // per, 0, 0))
    return pl.pallas_call(
        _post_kernel,
        grid=(t // tm,),
        in_specs=[row, row, vec, mod, vec, mod, mod],
        out_specs=[row, row],
        out_shape=[jax.ShapeDtypeStruct((t, d), F32), jax.ShapeDtypeStruct((t, d), BF16)],
        compiler_params=_params(("arbitrary",)),
        name="post_attn_norm",
    )(x2, y, g_post.reshape(1, d), gate, g_pre.reshape(1, d), scale, shift)


def _final_kernel(x_ref, y_ref, g_ref, gate_ref, o_ref):
    o_ref[...] = x_ref[...] + gate_ref[0] * _rms(y_ref[...], g_ref[...])


def _final(x1, y, g, gate, seq):
    t, d = x1.shape
    tm = min(256, seq)
    per = seq // tm
    row = pl.BlockSpec((tm, d), lambda i: (i, 0))
    return pl.pallas_call(
        _final_kernel,
        grid=(t // tm,),
        in_specs=[row, row, pl.BlockSpec((1, d), lambda i: (0, 0)),
                  pl.BlockSpec((1, 1, d), lambda i: (i // per, 0, 0))],
        out_specs=row,
        out_shape=jax.ShapeDtypeStruct((t, d), F32),
        compiler_params=_params(("arbitrary",)),
        name="final_residual",
    )(x1, y, g.reshape(1, d), gate)


def _mm_kernel(a_ref, w_ref, o_ref):
    o_ref[...] = jnp.dot(a_ref[...], w_ref[...], preferred_element_type=F32).astype(o_ref.dtype)


def _matmul(a, w, out_dtype, tn, tm=1024, name="matmul"):
    m, k = a.shape
    n = w.shape[1]
    tm = min(tm, m)
    tn = min(tn, n)
    return pl.pallas_call(
        _mm_kernel,
        grid=(m // tm, n // tn),
        in_specs=[pl.BlockSpec((tm, k), lambda i, j: (i, 0)),
                  pl.BlockSpec((k, tn), lambda i, j: (0, j))],
        out_specs=pl.BlockSpec((tm, tn), lambda i, j: (i, j)),
        out_shape=jax.ShapeDtypeStruct((m, n), out_dtype),
        compiler_params=_params(("arbitrary", "arbitrary")),
        name=name,
    )(a, w)


def _rope_upper(hi, cos_ref, sin_hi_ref, sin_lo_ref):
    return (hi * cos_ref[...]
            + pltpu.roll(hi, 32, axis=1) * sin_hi_ref[...]
            + pltpu.roll(hi, 96, axis=1) * sin_lo_ref[...])


def _mla_q_kernel(z_ref, g_ref, w_ref, cos_ref, s1_ref, s2_ref, o_ref, *, scale, heads):
    qn = _rms(z_ref[...], g_ref[...]).astype(BF16)
    y = jnp.dot(qn, w_ref[...], preferred_element_type=F32)
    for h in range(heads):
        lo = y[:, h * MLA_QK_PAD:h * MLA_QK_PAD + LANES]
        hi = y[:, h * MLA_QK_PAD + LANES:(h + 1) * MLA_QK_PAD]
        o_ref[:, h * MLA_QK_PAD:h * MLA_QK_PAD + LANES] = (lo * scale).astype(o_ref.dtype)
        o_ref[:, h * MLA_QK_PAD + LANES:(h + 1) * MLA_QK_PAD] = (
            _rope_upper(hi, cos_ref, s1_ref, s2_ref) * scale).astype(o_ref.dtype)


def _mla_q(z_a, g_q, w_q, rope, seq, q_rank, n_heads):
    t = z_a.shape[0]
    tm = min(512, seq)
    hb = min(4, n_heads)
    per = seq // tm
    scale = (MLA_NOPE_DIM + MLA_ROPE_DIM) ** -0.5 * math.log2(math.e)
    tab = pl.BlockSpec((tm, LANES), lambda i, j: (i % per, 0))
    return pl.pallas_call(
        functools.partial(_mla_q_kernel, scale=scale, heads=hb),
        grid=(t // tm, n_heads // hb),
        in_specs=[pl.BlockSpec((tm, q_rank), lambda i, j: (i, 0)),
                  pl.BlockSpec((1, q_rank), lambda i, j: (0, 0)),
                  pl.BlockSpec((q_rank, hb * MLA_QK_PAD), lambda i, j: (0, j)),
                  tab, tab, tab],
        out_specs=pl.BlockSpec((tm, hb * MLA_QK_PAD), lambda i, j: (i, j)),
        out_shape=jax.ShapeDtypeStruct((t, n_heads * MLA_QK_PAD), BF16),
        compiler_params=_params(("arbitrary", "arbitrary")),
        name="mla_q_proj",
    )(z_a, g_q.reshape(1, q_rank), w_q, *rope)


def _mla_kv_kernel(c_ref, pe_ref, g_ref, wk_ref, wv_ref, cos_ref, s1_ref, s2_ref,
                   k_ref, v_ref, *, heads):
    c = _rms(c_ref[...], g_ref[...]).astype(BF16)
    kn = jnp.dot(c, wk_ref[...], preferred_element_type=F32)
    v_ref[...] = jnp.dot(c, wv_ref[...], preferred_element_type=F32).astype(v_ref.dtype)
    pe = _rope_upper(pe_ref[...], cos_ref, s1_ref, s2_ref).astype(k_ref.dtype)
    for h in range(heads):
        k_ref[:, h * MLA_QK_PAD:h * MLA_QK_PAD + LANES] = (
            kn[:, h * LANES:(h + 1) * LANES].astype(k_ref.dtype))
        k_ref[:, h * MLA_QK_PAD + LANES:(h + 1) * MLA_QK_PAD] = pe


def _mla_kv(z_a, g_kv, w_kn, w_v, rope, seq, q_rank, kv_rank, n_heads):
    t = z_a.shape[0]
    tm = min(256, seq)
    per = seq // tm
    tab = pl.BlockSpec((tm, LANES), lambda i: (i % per, 0))
    full = lambda shape: pl.BlockSpec(shape, lambda i: (0, 0))
    return pl.pallas_call(
        functools.partial(_mla_kv_kernel, heads=n_heads),
        grid=(t // tm,),
        in_specs=[pl.BlockSpec((tm, kv_rank), lambda i: (i, q_rank // kv_rank)),
                  pl.BlockSpec((tm, LANES), lambda i: (i, (q_rank + kv_rank) // LANES)),
                  full((1, kv_rank)),
                  full((kv_rank, n_heads * MLA_NOPE_DIM)),
                  full((kv_rank, n_heads * MLA_V_DIM)),
                  tab, tab, tab],
        out_specs=[pl.BlockSpec((tm, n_heads * MLA_QK_PAD), lambda i: (i, 0)),
                   pl.BlockSpec((tm, n_heads * MLA_V_DIM), lambda i: (i, 0))],
        out_shape=[jax.ShapeDtypeStruct((t, n_heads * MLA_QK_PAD), BF16),
                   jax.ShapeDtypeStruct((t, n_heads * MLA_V_DIM), BF16)],
        compiler_params=_params(("arbitrary",)),
        name="mla_kv_proj",
    )(z_a, z_a, g_kv.reshape(1, kv_rank), w_kn, w_v, *rope)


def _mla_attn_kernel(q_ref, k_ref, v_ref, za_ref, o_ref, s_ref, m_ref, l_ref, acc_ref, *, tq):
    qi = pl.program_id(2)

    def put_scores(slot, j):
        kb = k_ref[pl.ds(pl.multiple_of(j * tq, tq), tq), :]
        s_ref[slot] = lax.dot_general(q_ref[...], kb, (((1,), (1,)), ((), ())),
                                      preferred_element_type=F32)

    def update(s, j):
        m = m_ref[...]
        m_new = jnp.maximum(m, jnp.max(s, axis=-1, keepdims=True))
        alpha = jnp.exp2(m - m_new)
        p = jnp.exp2(s - m_new)
        vb = v_ref[pl.ds(pl.multiple_of(j * tq, tq), tq), :]
        acc_ref[...] = alpha * acc_ref[...] + jnp.dot(p.astype(BF16), vb, preferred_element_type=F32)
        l_ref[...] = alpha * l_ref[...] + jnp.sum(p, axis=-1, keepdims=True)
        m_ref[...] = m_new

    def diagonal(slot):
        row = lax.broadcasted_iota(jnp.int32, (tq, tq), 0)
        col = lax.broadcasted_iota(jnp.int32, (tq, tq), 1)
        update(jnp.where(col <= row, s_ref[slot], NEG_INF), qi)

    m_ref[...] = jnp.full(m_ref.shape, NEG_INF, F32)
    l_ref[...] = jnp.zeros(l_ref.shape, F32)
    acc_ref[...] = jnp.zeros(acc_ref.shape, F32)
    put_scores(0, 0)

    def pair(jj, _):
        j = 2 * jj
        put_scores(1, j + 1)
        update(s_ref[0], j)
        put_scores(0, j + 2)
        update(s_ref[1], j + 1)
        return 0

    lax.fori_loop(0, qi // 2, pair, 0)

    @pl.when(qi % 2 == 1)
    def _():
        put_scores(1, qi)
        update(s_ref[0], qi - 1)
        diagonal(1)

    @pl.when(qi % 2 == 0)
    def _():
        diagonal(0)

    o_ref[...] = (acc_ref[...] / l_ref[...] * jax.nn.sigmoid(za_ref[...])).astype(o_ref.dtype)


def _mla_attn(q, k, v, z_g, batch, seq, n_heads):
    t = q.shape[0]
    tq = min(512, seq)
    nq = seq // tq
    return pl.pallas_call(
        functools.partial(_mla_attn_kernel, tq=tq),
        grid=(batch, n_heads, nq),
        in_specs=[pl.BlockSpec((tq, MLA_QK_PAD), lambda b, h, i: (b * nq + i, h)),
                  pl.BlockSpec((seq, MLA_QK_PAD), lambda b, h, i: (b, h)),
                  pl.BlockSpec((seq, MLA_V_DIM), lambda b, h, i: (b, h)),
                  pl.BlockSpec((tq, MLA_V_DIM), lambda b, h, i: (b * nq + i, h))],
        out_specs=pl.BlockSpec((tq, MLA_V_DIM), lambda b, h, i: (b * nq + i, h)),
        out_shape=jax.ShapeDtypeStruct((t, n_heads * MLA_V_DIM), BF16),
        scratch_shapes=[pltpu.VMEM((2, tq, tq), F32), pltpu.VMEM((tq, 1), F32),
                        pltpu.VMEM((tq, 1), F32), pltpu.VMEM((tq, MLA_V_DIM), F32)],
        compiler_params=_params(("arbitrary", "arbitrary", "arbitrary")),
        name="mla_flash_attention",
    )(q, k, v, z_g)


def _swa_kernel(sink_ref, q_ref, kp_ref, kc_ref, vp_ref, vc_ref, zb_ref, a_ref, o_ref,
                *, per, n_heads):
    i = pl.program_id(0)
    has_prev = (i % per) > 0
    group = n_heads // SWA_KV_HEADS
    hd = SWA_HEAD_DIM
    log2e = math.log2(math.e)
    scale = hd ** -0.5 * log2e
    span = 2 * WINDOW
    row = lax.broadcasted_iota(jnp.int32, (WINDOW, span), 0)
    col = lax.broadcasted_iota(jnp.int32, (WINDOW, span), 1)
    dist = row + WINDOW - col
    valid = (dist >= 0) & (dist < WINDOW) & ((col >= WINDOW) | has_prev)
    distf = dist.astype(F32)
    dn = (((1,), (1,)), ((), ()))
    for n in range(SWA_KV_HEADS):
        ksl = slice(n * hd, (n + 1) * hd)
        kb = jnp.concatenate([kp_ref[:, ksl], kc_ref[:, ksl]], axis=0).astype(BF16)
        vb = jnp.concatenate([vp_ref[:, ksl], vc_ref[:, ksl]], axis=0).astype(BF16)
        logits, sinks = [], []
        for g in range(group):
            hq = n * group + g
            slope = 2.0 ** (-8.0 * (hq + 1) / n_heads) * log2e
            qh = (q_ref[:, hq * hd:(hq + 1) * hd] * scale).astype(BF16)
            s = lax.dot_general(qh, kb, dn, preferred_element_type=F32)
            logits.append(jnp.where(valid, s - slope * distf, NEG_INF))
            sinks.append(jnp.full((WINDOW, 1), sink_ref[hq] * log2e, F32))
        s = jnp.concatenate(logits, axis=0)
        sink = jnp.concatenate(sinks, axis=0)
        m = jnp.maximum(jnp.max(s, axis=-1, keepdims=True), sink)
        p = jnp.exp2(s - m)
        den = jnp.sum(p, axis=-1, keepdims=True) + jnp.exp2(sink - m)
        o = jnp.dot(p.astype(BF16), vb, preferred_element_type=F32) / den
        for g in range(0, group, 2):
            hq = n * group + g
            sl = slice(hq * hd, (hq + 2) * hd)
            o2 = jnp.concatenate([o[g * WINDOW:(g + 1) * WINDOW],
                                  o[(g + 1) * WINDOW:(g + 2) * WINDOW]], axis=1)
            mix = a_ref[:, sl].astype(F32) + jax.nn.sigmoid(zb_ref[:, sl]) * o2
            o_ref[:, sl] = mix.astype(o_ref.dtype)


def _swa(z_swa, z_g, gated_a, sinks, seq, d):
    t = z_swa.shape[0]
    n_heads = d // SWA_HEAD_DIM
    kvw = SWA_KV_HEADS * SWA_HEAD_DIM
    per = seq // WINDOW
    kcol = d // kvw
    prev = lambda c: (lambda i: (jnp.maximum(i - 1, 0), c))
    cur = lambda c: (lambda i: (i, c))
    return pl.pallas_call(
        functools.partial(_swa_kernel, per=per, n_heads=n_heads),
        grid=(t // WINDOW,),
        in_specs=[pl.BlockSpec(memory_space=pltpu.SMEM),
                  pl.BlockSpec((WINDOW, d), cur(0)),
                  pl.BlockSpec((WINDOW, kvw), prev(kcol)),
                  pl.BlockSpec((WINDOW, kvw), cur(kcol)),
                  pl.BlockSpec((WINDOW, kvw), prev(kcol + 1)),
                  pl.BlockSpec((WINDOW, kvw), cur(kcol + 1)),
                  pl.BlockSpec((WINDOW, d), cur(1)),
                  pl.BlockSpec((WINDOW, d), cur(0))],
        out_specs=pl.BlockSpec((WINDOW, d), cur(0)),
        out_shape=jax.ShapeDtypeStruct((t, d), BF16),
        compiler_params=_params(("arbitrary",)),
        name="swa_sink_attention",
    )(sinks, z_swa, z_swa, z_swa, z_swa, z_swa, z_g, gated_a)


def _top16(s):
    rows = lax.broadcasted_iota(jnp.int32, s.shape, 0).astype(F32)
    rank = jnp.full(s.shape, float(PEER_TOPK), F32)
    vals = []
    for r in range(PEER_TOPK):
        m = jnp.max(s, axis=0, keepdims=True)
        first = jnp.min(jnp.where(s == m, rows, float(PEER_N_KEYS)), axis=0, keepdims=True)
        hit = rows == first
        rank = jnp.where(hit, float(r), rank)
        s = jnp.where(hit, NEG_INF, s)
        vals.append(m)
    return vals, rank


def _peer_select(s1, s2):
    k = PEER_TOPK
    v1, rank1 = _top16(s1)
    v2, rank2 = _top16(s2)
    n = s1.shape[1]
    hk = k // 2
    r8 = lax.broadcasted_iota(jnp.int32, (hk, n), 0).astype(F32)

    def stack(vals):
        out = jnp.zeros((hk, n), F32)
        for i, v in enumerate(vals):
            out = jnp.where(r8 == float(i), v, out)
        return out

    v2_lo, v2_hi, v1_hi = stack(v2[:hk]), stack(v2[hk:]), stack(v1[hk:])
    cand = [v1[0] + v2_lo, v1[0] + v2_hi]
    code = [r8, r8 + float(hk)]
    for a in range(1, hk):
        cand.append(jnp.where(r8 < float(k // (a + 1)), v1[a] + v2_lo, NEG_INF))
        code.append(r8 + float(a * k))
    cand.append(v1_hi + v2[0])
    code.append((r8 + float(hk)) * float(k))
    sel = [jnp.zeros((hk, n), F32) for _ in cand]
    for _ in range(k):
        m = jnp.max(functools.reduce(jnp.maximum, cand), axis=0, keepdims=True)
        first = functools.reduce(
            jnp.minimum, [jnp.where(cv == m, cd, float(k * k)) for cv, cd in zip(cand, code)])
        first = jnp.min(first, axis=0, keepdims=True)
        for i in range(len(cand)):
            hit = code[i] == first
            sel[i] = jnp.where(hit, 1.0, sel[i])
            cand[i] = jnp.where(hit, NEG_INF, cand[i])
    e2_lo = jnp.exp(v2_lo - v2[0])
    e2_hi = jnp.exp(v2_hi - v2[0])
    colsum = lambda x: jnp.sum(x, axis=0, keepdims=True)
    cnt = [colsum(sel[0]) + colsum(sel[1])] + [colsum(sel[a + 1]) for a in range(1, hk)]
    mass = [colsum(sel[0] * e2_lo) + colsum(sel[1] * e2_hi)]
    mass += [colsum(sel[a + 1] * e2_lo) for a in range(1, hk)]
    z = jnp.zeros((1, n), F32)
    n1 = jnp.zeros(s1.shape, F32)
    for a in range(hk):
        z = z + jnp.exp(v1[a] - v1[0]) * mass[a]
        n1 = jnp.where(rank1 == float(a), cnt[a], n1)
    z = z + colsum(sel[-1] * jnp.exp(v1_hi - v1[0]))
    for a in range(hk, k):
        n1 = jnp.where(rank1 == float(a), sel[-1][a - hk:a - hk + 1, :], n1)
    e1 = jnp.exp(s1 - v1[0])
    e2 = jnp.exp(s2 - v2[0]) / z
    return n1, e1, rank2, e2


def _peer_topk_kernel(q_ref, key_ref, n1_ref, e1_ref, b2_ref, e2_ref, *, sub):
    tm = q_ref.shape[0]
    half = q_ref.shape[1] // 2
    dn = (((1,), (1,)), ((), ()))
    for c in range(tm // sub):
        cs = slice(c * sub, (c + 1) * sub)
        q1 = q_ref[cs, :half]
        q2 = q_ref[cs, half:]
        s1 = lax.dot_general(key_ref[0, 0], q1, dn, preferred_element_type=F32)
        s2 = lax.dot_general(key_ref[0, 1], q2, dn, preferred_element_type=F32)
        n1, e1, b2, e2 = _peer_select(s1, s2)
        n1_ref[:, cs] = n1
        e1_ref[:, cs] = e1
        b2_ref[:, cs] = b2.astype(b2_ref.dtype)
        e2_ref[:, cs] = e2.astype(e2_ref.dtype)


def _peer_topk(q, keys):
    t = q.shape[0]
    heads = keys.shape[0]
    qd = q.shape[1] // heads
    tm = min(256, t)
    out = pl.BlockSpec((PEER_N_KEYS, tm), lambda i, h: (h, i))
    shape = lambda dt: jax.ShapeDtypeStruct((heads * PEER_N_KEYS, t), dt)
    return pl.pallas_call(
        functools.partial(_peer_topk_kernel, sub=min(128, tm)),
        grid=(t // tm, heads),
        in_specs=[pl.BlockSpec((tm, qd), lambda i, h: (i, h)),
                  pl.BlockSpec((1, 2, PEER_N_KEYS, qd // 2), lambda i, h: (h, 0, 0, 0))],
        out_specs=[out, out, out, out],
        out_shape=[shape(F32), shape(F32), shape(BF16), shape(BF16)],
        compiler_params=_params(("arbitrary", "arbitrary")),
        name="peer_topk",
    )(q, keys)


def _peer_dense_kernel(ht_ref, u_ref, vt_ref, n1_ref, e1_ref, b2_ref, e2_ref, o_ref, a_ref, *, heads):
    c = pl.program_id(1)
    tn = u_ref.shape[0]
    nk = PEER_N_KEYS

    @pl.when(c == 0)
    def _():
        o_ref[...] = jnp.zeros_like(o_ref)

    hid = jnp.dot(u_ref[...], ht_ref[...], preferred_element_type=F32)
    act = (0.5 * hid * (1.0 + lax.erf(hid * (2.0 ** -0.5)))).astype(BF16)
    for k in range(tn // nk):
        gate = None
        for h in range(heads):
            r = k * heads + h
            n1 = n1_ref[r:r + 1, :].astype(BF16)
            e1 = e1_ref[r:r + 1, :].astype(BF16)
            b2 = b2_ref[h * nk:(h + 1) * nk, :]
            e2 = e2_ref[h * nk:(h + 1) * nk, :]
            term = jnp.where(b2 < n1, e2, jnp.zeros_like(e2)) * e1
            gate = term if gate is None else gate + term
        a_ref[k * nk:(k + 1) * nk, :] = act[k * nk:(k + 1) * nk, :] * gate
    o_ref[...] += jnp.dot(vt_ref[...], a_ref[...], preferred_element_type=F32)


def _peer_dense(ht, u, vt, n1, e1, b2, e2, heads):
    d, t = ht.shape
    n_exp = u.shape[0]
    tm = min(512, t)
    tn = min(512, n_exp)
    rows = heads * PEER_N_KEYS
    crow = (tn // PEER_N_KEYS) * heads
    sel = pl.BlockSpec((rows, tm), lambda i, c: (0, i))
    csel = pl.BlockSpec((crow, tm), lambda i, c: (c, i))
    return pl.pallas_call(
        functools.partial(_peer_dense_kernel, heads=heads),
        grid=(t // tm, n_exp // tn),
        in_specs=[pl.BlockSpec((d, tm), lambda i, c: (0, i)),
                  pl.BlockSpec((tn, d), lambda i, c: (c, 0)),
                  pl.BlockSpec((d, tn), lambda i, c: (0, c)),
                  csel, csel, sel, sel],
        out_specs=pl.BlockSpec((d, tm), lambda i, c: (0, i)),
        out_shape=jax.ShapeDtypeStruct((d, t), F32),
        scratch_shapes=[pltpu.VMEM((tn, tm), BF16)],
        compiler_params=_params(("arbitrary", "arbitrary")),
        name="peer_dense_experts",
    )(ht, u, vt, n1, e1, b2, e2)


def _rope_tables(seq):
    half = MLA_ROPE_DIM // 2
    pos = jnp.arange(seq, dtype=F32)
    inv_freq = ROPE_THETA ** (-jnp.arange(0, MLA_ROPE_DIM, 2, dtype=F32) / MLA_ROPE_DIM)
    ang = pos[:, None] * inv_freq[None, :]
    cos, sin = jnp.cos(ang), jnp.sin(ang)
    zero = jnp.zeros((seq, half), F32)
    pad = jnp.zeros((seq, LANES - MLA_ROPE_DIM), F32)
    cos_t = jnp.concatenate([cos, cos, pad], axis=1)
    sin_hi = jnp.concatenate([zero, sin, pad], axis=1)
    sin_lo = jnp.concatenate([-sin, zero, pad], axis=1)
    return cos_t, sin_hi, sin_lo


def _layer(x2, cond_in, batch, seq, w_ada, b_ada, g_attn_pre, g_attn_post, w_in, g_q_lat, w_q_b,
           g_kv_lat, w_kv_b, sinks, w_out, g_ffn_pre, g_ffn_post, w_peer_q, sub_keys, peer_u, peer_v):
    t, d = x2.shape
    q_rank = g_q_lat.shape[0]
    kv_rank = g_kv_lat.shape[0]
    n_mla = d // MLA_V_DIM
    kvw = SWA_KV_HEADS * SWA_HEAD_DIM
    heads = sub_keys.shape[0]

    ada = _ada(cond_in, w_ada, b_ada).reshape(batch, N_ADA, 1, d)
    shift_a, scale_a, gate_a, shift_f, scale_f, gate_f = [ada[:, i] for i in range(N_ADA)]

    w_a_cols = q_rank + kv_rank + MLA_ROPE_DIM
    w_a = jnp.pad(w_in[:, :w_a_cols], ((0, 0), (0, -w_a_cols % (2 * LANES)))).astype(BF16)
    off_swa = w_a_cols
    off_gate = off_swa + d + 2 * kvw
    w_swa = w_in[:, off_swa:off_gate].astype(BF16)
    w_g = w_in[:, off_gate:].astype(BF16)
    qk = MLA_NOPE_DIM + MLA_ROPE_DIM
    w_q = jnp.pad(w_q_b.reshape(q_rank, n_mla, qk),
                  ((0, 0), (0, 0), (0, MLA_QK_PAD - qk))).reshape(q_rank, n_mla * MLA_QK_PAD).astype(BF16)
    w_kv = w_kv_b.reshape(kv_rank, n_mla, MLA_NOPE_DIM + MLA_V_DIM)
    w_kn = w_kv[:, :, :MLA_NOPE_DIM].reshape(kv_rank, n_mla * MLA_NOPE_DIM).astype(BF16)
    w_v = w_kv[:, :, MLA_NOPE_DIM:].reshape(kv_rank, n_mla * MLA_V_DIM).astype(BF16)
    rope = _rope_tables(seq)

    h = _prenorm(x2, g_attn_pre, scale_a, shift_a, seq)
    z_a = _matmul(h, w_a, F32, tn=w_a.shape[1] // 2, name="in_proj_mla")
    z_swa = _matmul(h, w_swa, F32, tn=512, name="in_proj_swa")
    z_g = _matmul(h, w_g, F32, tn=512, name="in_proj_gates")
    q = _mla_q(z_a, g_q_lat, w_q, rope, seq, q_rank, n_mla)
    k, v = _mla_kv(z_a, g_kv_lat, w_kn, w_v, rope, seq, q_rank, kv_rank, n_mla)
    gated_a = _mla_attn(q, k, v, z_g, batch, seq, n_mla)
    mix = _swa(z_swa, z_g, gated_a, sinks, seq, d)
    y = _matmul(mix, w_out.astype(BF16), F32, tn=512, name="out_proj")
    x1, h2 = _post_attn(x2, y, g_attn_post, gate_a, g_ffn_pre, scale_f, shift_f, seq)

    pq = _matmul(h2, w_peer_q.astype(BF16), BF16, tn=512, name="peer_query")
    n1, e1, b2, e2 = _peer_topk(pq, sub_keys.astype(BF16))
    by_i1 = lambda a: a.reshape(heads, PEER_N_KEYS, t).transpose(1, 0, 2).reshape(heads * PEER_N_KEYS, t)
    n1, e1 = by_i1(n1), by_i1(e1)
    yt =_peer_dense(h2.T, peer_u.astype(BF16), peer_v.T.astype(BF16), n1, e1, b2, e2, heads)
    return _final(x1, yt.T, g_ffn_post, gate_f, seq)


def kernel(x, c, w_ada, b_ada, g_attn_pre, g_attn_post, w_in, g_q_lat, w_q_b, g_kv_lat, w_kv_b, sinks, w_out, g_ffn_pre, g_ffn_post, w_peer_q, peer_sub_keys, peer_u, peer_v):
    batch, seq, d = x.shape
    x2 = x.reshape(batch * seq, d)
    for l in range(w_ada.shape[0]):
        x2 = _layer(x2, c, batch, seq, w_ada[l], b_ada[l], g_attn_pre[l], g_attn_post[l], w_in[l],
                    g_q_lat[l], w_q_b[l], g_kv_lat[l], w_kv_b[l], sinks[l], w_out[l], g_ffn_pre[l],
                    g_ffn_post[l], w_peer_q[l], peer_sub_keys[l], peer_u[l], peer_v[l])
    return x2.reshape(batch, seq, d)
```

```python
import functools
import math

import jax
import jax.numpy as jnp
from jax import lax
from jax.experimental import pallas as pl
from jax.experimental.pallas import tpu as pltpu

F32 = jnp.float32
BF16 = jnp.bfloat16

MLA_NOPE_DIM = 128
MLA_ROPE_DIM = 64
MLA_V_DIM = 128
MLA_QK_PAD = 256
ROPE_THETA = 10000.0
SWA_HEAD_DIM = 64
SWA_KV_HEADS = 8
WINDOW = 128
PEER_HEADS = 8
PEER_N_KEYS = 128
PEER_TOPK = 16
N_ADA = 6
NORM_EPS = 1e-6
NEG_INF = -1e30

LANES = 128
VMEM_LIMIT = 56 * 1024 * 1024
VMEM_LIMIT_LARGE = 60 * 1024 * 1024


def _params(sem, vmem=VMEM_LIMIT):
    return pltpu.CompilerParams(dimension_semantics=sem, vmem_limit_bytes=vmem)


def _rms(x, g):
    return x * lax.rsqrt(jnp.mean(x * x, axis=-1, keepdims=True) + NORM_EPS) * g


def _ada_kernel(cb_ref, w_ref, b_ref, o_ref):
    nb = cb_ref.shape[0]
    tn = w_ref.shape[1]
    for b in range(nb):
        cb = cb_ref[b]
        cond = cb * jax.nn.sigmoid(cb)
        for j in range(tn // LANES):
            sl = slice(j * LANES, (j + 1) * LANES)
            r = jnp.sum(w_ref[:, sl] * cond, axis=0, keepdims=True)
            o_ref[b:b + 1, sl] = r + b_ref[:, sl]


def _ada(c, w, bias):
    nb, k = c.shape
    n = w.shape[1]
    tn = min(512, n)
    cb = jnp.broadcast_to(c[:, :, None], (nb, k, LANES))
    return pl.pallas_call(
        _ada_kernel,
        grid=(n // tn,),
        in_specs=[pl.BlockSpec((nb, k, LANES), lambda j: (0, 0, 0)),
                  pl.BlockSpec((k, tn), lambda j: (0, j)),
                  pl.BlockSpec((1, tn), lambda j: (0, j))],
        out_specs=pl.BlockSpec((nb, tn), lambda j: (0, j)),
        out_shape=jax.ShapeDtypeStruct((nb, n), F32),
        compiler_params=_params(("arbitrary",)),
        name="ada_matvec",
    )(cb, w, bias.reshape(1, n))


def _prenorm_kernel(x_ref, g_ref, sc_ref, sh_ref, o_ref):
    h = _rms(x_ref[...], g_ref[...]) * (1.0 + sc_ref[0]) + sh_ref[0]
    o_ref[...] = h.astype(o_ref.dtype)


def _prenorm(x2, g, scale, shift, seq):
    t, d = x2.shape
    tm = min(256, seq)
    per = seq // tm
    row = pl.BlockSpec((tm, d), lambda i: (i, 0))
    mod = pl.BlockSpec((1, 1, d), lambda i: (i // per, 0, 0))
    return pl.pallas_call(
        _prenorm_kernel,
        grid=(t // tm,),
        in_specs=[row, pl.BlockSpec((1, d), lambda i: (0, 0)), mod, mod],
        out_specs=row,
        out_shape=jax.ShapeDtypeStruct((t, d), BF16),
        compiler_params=_params(("arbitrary",)),
        name="prenorm_modulate",
    )(x2, g.reshape(1, d), scale, shift)


def _post_kernel(x_ref, y_ref, gpost_ref, gate_ref, gpre_ref, sc_ref, sh_ref, x1_ref, h_ref):
    x1 = x_ref[...] + gate_ref[0] * _rms(y_ref[...], gpost_ref[...])
    x1_ref[...] = x1
    h = _rms(x1, gpre_ref[...]) * (1.0 + sc_ref[0]) + sh_ref[0]
    h_ref[...] = h.T.astype(h_ref.dtype)


def _post_attn(x2, y, g_post, gate, g_pre, scale, shift, seq):
    t, d = x2.shape
    tm = min(256, seq)
    per = seq // tm
    row = pl.BlockSpec((tm, d), lambda i: (i, 0))
    vec = pl.BlockSpec((1, d), lambda i: (0, 0))
    mod = pl.BlockSpec((1, 1, d), lambda i: (i // per, 0, 0))
    return pl.pallas_call(
        _post_kernel,
        grid=(t // tm,),
        in_specs=[row, row, vec, mod, vec, mod, mod],
        out_specs=[row, pl.BlockSpec((d, tm), lambda i: (0, i))],
        out_shape=[jax.ShapeDtypeStruct((t, d), F32), jax.ShapeDtypeStruct((d, t), BF16)],
        compiler_params=_params(("arbitrary",)),
        name="post_attn_norm",
    )(x2, y, g_post.reshape(1, d), gate, g_pre.reshape(1, d), scale, shift)


def _final_kernel(x_ref, yt_ref, g_ref, gate_ref, o_ref):
    o_ref[...] = x_ref[...] + gate_ref[0] * _rms(yt_ref[...].T, g_ref[...])


def _final(x1, yt, g, gate, seq):
    t, d = x1.shape
    tm = min(256, seq)
    per = seq // tm
    row = pl.BlockSpec((tm, d), lambda i: (i, 0))
    return pl.pallas_call(
        _final_kernel,
        grid=(t // tm,),
        in_specs=[row, pl.BlockSpec((d, tm), lambda i: (0, i)), pl.BlockSpec((1, d), lambda i: (0, 0)),
                  pl.BlockSpec((1, 1, d), lambda i: (i // per, 0, 0))],
        out_specs=row,
        out_shape=jax.ShapeDtypeStruct((t, d), F32),
        compiler_params=_params(("arbitrary",)),
        name="final_residual",
    )(x1, yt, g.reshape(1, d), gate)


def _mm_kernel(a_ref, w_ref, o_ref):
    o_ref[...] = jnp.dot(a_ref[...], w_ref[...], preferred_element_type=F32).astype(o_ref.dtype)


def _matmul(a, w, out_dtype, tn, tm=1024, name="matmul"):
    m, k = a.shape
    n = w.shape[1]
    tm = min(tm, m)
    tn = min(tn, n)
    return pl.pallas_call(
        _mm_kernel,
        grid=(m // tm, n // tn),
        in_specs=[pl.BlockSpec((tm, k), lambda i, j: (i, 0)),
                  pl.BlockSpec((k, tn), lambda i, j: (0, j))],
        out_specs=pl.BlockSpec((tm, tn), lambda i, j: (i, j)),
        out_shape=jax.ShapeDtypeStruct((m, n), out_dtype),
        compiler_params=_params(("arbitrary", "arbitrary")),
        name=name,
    )(a, w)


def _rope_upper(hi, cos_ref, sin_hi_ref, sin_lo_ref):
    return (hi * cos_ref[...]
            + pltpu.roll(hi, 32, axis=1) * sin_hi_ref[...]
            + pltpu.roll(hi, 96, axis=1) * sin_lo_ref[...])


def _mla_q_kernel(z_ref, g_ref, w_ref, cos_ref, s1_ref, s2_ref, o_ref, *, scale, heads):
    qn = _rms(z_ref[...], g_ref[...]).astype(BF16)
    y = jnp.dot(qn, w_ref[...], preferred_element_type=F32)
    for h in range(heads):
        lo = y[:, h * MLA_QK_PAD:h * MLA_QK_PAD + LANES]
        hi = y[:, h * MLA_QK_PAD + LANES:(h + 1) * MLA_QK_PAD]
        o_ref[:, h * MLA_QK_PAD:h * MLA_QK_PAD + LANES] = (lo * scale).astype(o_ref.dtype)
        o_ref[:, h * MLA_QK_PAD + LANES:(h + 1) * MLA_QK_PAD] = (
            _rope_upper(hi, cos_ref, s1_ref, s2_ref) * scale).astype(o_ref.dtype)


def _mla_q(z_a, g_q, w_q, rope, seq, q_rank, n_heads):
    t = z_a.shape[0]
    tm = min(512, seq)
    hb = min(4, n_heads)
    per = seq // tm
    scale = (MLA_NOPE_DIM + MLA_ROPE_DIM) ** -0.5 * math.log2(math.e)
    tab = pl.BlockSpec((tm, LANES), lambda i, j: (i % per, 0))
    return pl.pallas_call(
        functools.partial(_mla_q_kernel, scale=scale, heads=hb),
        grid=(t // tm, n_heads // hb),
        in_specs=[pl.BlockSpec((tm, q_rank), lambda i, j: (i, 0)),
                  pl.BlockSpec((1, q_rank), lambda i, j: (0, 0)),
                  pl.BlockSpec((q_rank, hb * MLA_QK_PAD), lambda i, j: (0, j)),
                  tab, tab, tab],
        out_specs=pl.BlockSpec((tm, hb * MLA_QK_PAD), lambda i, j: (i, j)),
        out_shape=jax.ShapeDtypeStruct((t, n_heads * MLA_QK_PAD), BF16),
        compiler_params=_params(("arbitrary", "arbitrary")),
        name="mla_q_proj",
    )(z_a, g_q.reshape(1, q_rank), w_q, *rope)


def _mla_kv_kernel(c_ref, pe_ref, g_ref, wk_ref, wv_ref, cos_ref, s1_ref, s2_ref,
                   k_ref, v_ref, *, heads):
    c = _rms(c_ref[...], g_ref[...]).astype(BF16)
    kn = jnp.dot(c, wk_ref[...], preferred_element_type=F32)
    v_ref[...] = jnp.dot(c, wv_ref[...], preferred_element_type=F32).astype(v_ref.dtype)
    pe = _rope_upper(pe_ref[...], cos_ref, s1_ref, s2_ref).astype(k_ref.dtype)
    for h in range(heads):
        k_ref[:, h * MLA_QK_PAD:h * MLA_QK_PAD + LANES] = (
            kn[:, h * LANES:(h + 1) * LANES].astype(k_ref.dtype))
        k_ref[:, h * MLA_QK_PAD + LANES:(h + 1) * MLA_QK_PAD] = pe


def _mla_kv(z_a, g_kv, w_kn, w_v, rope, seq, q_rank, kv_rank, n_heads):
    t = z_a.shape[0]
    tm = min(256, seq)
    per = seq // tm
    tab = pl.BlockSpec((tm, LANES), lambda i: (i % per, 0))
    full = lambda shape: pl.BlockSpec(shape, lambda i: (0, 0))
    return pl.pallas_call(
        functools.partial(_mla_kv_kernel, heads=n_heads),
        grid=(t // tm,),
        in_specs=[pl.BlockSpec((tm, kv_rank), lambda i: (i, q_rank // kv_rank)),
                  pl.BlockSpec((tm, LANES), lambda i: (i, (q_rank + kv_rank) // LANES)),
                  full((1, kv_rank)),
                  full((kv_rank, n_heads * MLA_NOPE_DIM)),
                  full((kv_rank, n_heads * MLA_V_DIM)),
                  tab, tab, tab],
        out_specs=[pl.BlockSpec((tm, n_heads * MLA_QK_PAD), lambda i: (i, 0)),
                   pl.BlockSpec((tm, n_heads * MLA_V_DIM), lambda i: (i, 0))],
        out_shape=[jax.ShapeDtypeStruct((t, n_heads * MLA_QK_PAD), BF16),
                   jax.ShapeDtypeStruct((t, n_heads * MLA_V_DIM), BF16)],
        compiler_params=_params(("arbitrary",)),
        name="mla_kv_proj",
    )(z_a, z_a, g_kv.reshape(1, kv_rank), w_kn, w_v, *rope)


def _mla_attn_kernel(q_ref, k_ref, v_ref, za_ref, o_ref, s_ref, m_ref, l_ref, acc_ref, *, tq):
    qi = pl.program_id(2)

    def put_scores(slot, j):
        kb = k_ref[pl.ds(pl.multiple_of(j * tq, tq), tq), :]
        s_ref[slot] = lax.dot_general(q_ref[...], kb, (((1,), (1,)), ((), ())),
                                      preferred_element_type=F32)

    def update(s, j):
        m = m_ref[...]
        m_new = jnp.maximum(m, jnp.max(s, axis=-1, keepdims=True))
        alpha = jnp.exp2(m - m_new)
        p = jnp.exp2(s - m_new)
        vb = v_ref[pl.ds(pl.multiple_of(j * tq, tq), tq), :]
        acc_ref[...] = alpha * acc_ref[...] + jnp.dot(p.astype(BF16), vb, preferred_element_type=F32)
        l_ref[...] = alpha * l_ref[...] + jnp.sum(p, axis=-1, keepdims=True)
        m_ref[...] = m_new

    def diagonal(slot):
        row = lax.broadcasted_iota(jnp.int32, (tq, tq), 0)
        col = lax.broadcasted_iota(jnp.int32, (tq, tq), 1)
        update(jnp.where(col <= row, s_ref[slot], NEG_INF), qi)

    m_ref[...] = jnp.full(m_ref.shape, NEG_INF, F32)
    l_ref[...] = jnp.zeros(l_ref.shape, F32)
    acc_ref[...] = jnp.zeros(acc_ref.shape, F32)
    put_scores(0, 0)

    def pair(jj, _):
        j = 2 * jj
        put_scores(1, j + 1)
        update(s_ref[0], j)
        put_scores(0, j + 2)
        update(s_ref[1], j + 1)
        return 0

    lax.fori_loop(0, qi // 2, pair, 0)

    @pl.when(qi % 2 == 1)
    def _():
        put_scores(1, qi)
        update(s_ref[0], qi - 1)
        diagonal(1)

    @pl.when(qi % 2 == 0)
    def _():
        diagonal(0)

    o_ref[...] = (acc_ref[...] / l_ref[...] * jax.nn.sigmoid(za_ref[...])).astype(o_ref.dtype)


def _mla_attn(q, k, v, z_g, batch, seq, n_heads):
    t = q.shape[0]
    tq = min(1024, seq)
    nq = seq // tq
    return pl.pallas_call(
        functools.partial(_mla_attn_kernel, tq=tq),
        grid=(batch, n_heads, nq),
        in_specs=[pl.BlockSpec((tq, MLA_QK_PAD), lambda b, h, i: (b * nq + i, h)),
                  pl.BlockSpec((seq, MLA_QK_PAD), lambda b, h, i: (b, h)),
                  pl.BlockSpec((seq, MLA_V_DIM), lambda b, h, i: (b, h)),
                  pl.BlockSpec((tq, MLA_V_DIM), lambda b, h, i: (b * nq + i, h))],
        out_specs=pl.BlockSpec((tq, MLA_V_DIM), lambda b, h, i: (b * nq + i, h)),
        out_shape=jax.ShapeDtypeStruct((t, n_heads * MLA_V_DIM), BF16),
        scratch_shapes=[pltpu.VMEM((2, tq, tq), F32), pltpu.VMEM((tq, 1), F32),
                        pltpu.VMEM((tq, 1), F32), pltpu.VMEM((tq, MLA_V_DIM), F32)],
        compiler_params=_params(("arbitrary", "arbitrary", "arbitrary")),
        name="mla_flash_attention",
    )(q, k, v, z_g)


def _swa_kernel(sink_ref, q_ref, kp_ref, kc_ref, vp_ref, vc_ref, zb_ref, a_ref, o_ref,
                *, per, n_heads):
    i = pl.program_id(0)
    has_prev = (i % per) > 0
    group = n_heads // SWA_KV_HEADS
    hd = SWA_HEAD_DIM
    log2e = math.log2(math.e)
    scale = hd ** -0.5 * log2e
    span = 2 * WINDOW
    row = lax.broadcasted_iota(jnp.int32, (WINDOW, span), 0)
    col = lax.broadcasted_iota(jnp.int32, (WINDOW, span), 1)
    dist = row + WINDOW - col
    valid = (dist >= 0) & (dist < WINDOW) & ((col >= WINDOW) | has_prev)
    distf = dist.astype(F32)
    dn = (((1,), (1,)), ((), ()))
    for n in range(SWA_KV_HEADS):
        ksl = slice(n * hd, (n + 1) * hd)
        kb = jnp.concatenate([kp_ref[:, ksl], kc_ref[:, ksl]], axis=0).astype(BF16)
        vb = jnp.concatenate([vp_ref[:, ksl], vc_ref[:, ksl]], axis=0).astype(BF16)
        logits, sinks = [], []
        for g in range(group):
            hq = n * group + g
            slope = 2.0 ** (-8.0 * (hq + 1) / n_heads) * log2e
            qh = (q_ref[:, hq * hd:(hq + 1) * hd] * scale).astype(BF16)
            s = lax.dot_general(qh, kb, dn, preferred_element_type=F32)
            logits.append(jnp.where(valid, s - slope * distf, NEG_INF))
            sinks.append(jnp.full((WINDOW, 1), sink_ref[hq] * log2e, F32))
        s = jnp.concatenate(logits, axis=0)
        sink = jnp.concatenate(sinks, axis=0)
        m = jnp.maximum(jnp.max(s, axis=-1, keepdims=True), sink)
        p = jnp.exp2(s - m)
        den = jnp.sum(p, axis=-1, keepdims=True) + jnp.exp2(sink - m)
        o = jnp.dot(p.astype(BF16), vb, preferred_element_type=F32) / den
        for g in range(0, group, 2):
            hq = n * group + g
            sl = slice(hq * hd, (hq + 2) * hd)
            o2 = jnp.concatenate([o[g * WINDOW:(g + 1) * WINDOW],
                                  o[(g + 1) * WINDOW:(g + 2) * WINDOW]], axis=1)
            mix = a_ref[:, sl].astype(F32) + jax.nn.sigmoid(zb_ref[:, sl]) * o2
            o_ref[:, sl] = mix.astype(o_ref.dtype)


def _swa(z_swa, z_g, gated_a, sinks, seq, d):
    t = z_swa.shape[0]
    n_heads = d // SWA_HEAD_DIM
    kvw = SWA_KV_HEADS * SWA_HEAD_DIM
    per = seq // WINDOW
    kcol = d // kvw
    prev = lambda c: (lambda i: (jnp.maximum(i - 1, 0), c))
    cur = lambda c: (lambda i: (i, c))
    return pl.pallas_call(
        functools.partial(_swa_kernel, per=per, n_heads=n_heads),
        grid=(t // WINDOW,),
        in_specs=[pl.BlockSpec(memory_space=pltpu.SMEM),
                  pl.BlockSpec((WINDOW, d), cur(0)),
                  pl.BlockSpec((WINDOW, kvw), prev(kcol)),
                  pl.BlockSpec((WINDOW, kvw), cur(kcol)),
                  pl.BlockSpec((WINDOW, kvw), prev(kcol + 1)),
                  pl.BlockSpec((WINDOW, kvw), cur(kcol + 1)),
                  pl.BlockSpec((WINDOW, d), cur(1)),
                  pl.BlockSpec((WINDOW, d), cur(0))],
        out_specs=pl.BlockSpec((WINDOW, d), cur(0)),
        out_shape=jax.ShapeDtypeStruct((t, d), BF16),
        compiler_params=_params(("arbitrary",)),
        name="swa_sink_attention",
    )(sinks, z_swa, z_swa, z_swa, z_swa, z_swa, z_g, gated_a)


def _top16(s):
    rows = lax.broadcasted_iota(jnp.int32, s.shape, 0).astype(F32)
    rank = jnp.full(s.shape, float(PEER_TOPK), F32)
    vals = []
    for r in range(PEER_TOPK):
        m = jnp.max(s, axis=0, keepdims=True)
        first = jnp.min(jnp.where(s == m, rows, float(PEER_N_KEYS)), axis=0, keepdims=True)
        hit = rows == first
        rank = jnp.where(hit, float(r), rank)
        s = jnp.where(hit, NEG_INF, s)
        vals.append(m)
    return vals, rank


def _peer_select(s1, s2):
    k = PEER_TOPK
    v1, rank1 = _top16(s1)
    v2, rank2 = _top16(s2)
    n = s1.shape[1]
    hk = k // 2
    r8 = lax.broadcasted_iota(jnp.int32, (hk, n), 0).astype(F32)

    def stack(vals):
        out = jnp.zeros((hk, n), F32)
        for i, v in enumerate(vals):
            out = jnp.where(r8 == float(i), v, out)
        return out

    v2_lo, v2_hi, v1_hi = stack(v2[:hk]), stack(v2[hk:]), stack(v1[hk:])
    cand = [v1[0] + v2_lo, v1[0] + v2_hi]
    code = [r8, r8 + float(hk)]
    for a in range(1, hk):
        cand.append(jnp.where(r8 < float(k // (a + 1)), v1[a] + v2_lo, NEG_INF))
        code.append(r8 + float(a * k))
    cand.append(v1_hi + v2[0])
    code.append((r8 + float(hk)) * float(k))
    sel = [jnp.zeros((hk, n), F32) for _ in cand]
    for _ in range(k):
        m = jnp.max(functools.reduce(jnp.maximum, cand), axis=0, keepdims=True)
        first = functools.reduce(
            jnp.minimum, [jnp.where(cv == m, cd, float(k * k)) for cv, cd in zip(cand, code)])
        first = jnp.min(first, axis=0, keepdims=True)
        for i in range(len(cand)):
            hit = code[i] == first
            sel[i] = jnp.where(hit, 1.0, sel[i])
            cand[i] = jnp.where(hit, NEG_INF, cand[i])
    e2_lo = jnp.exp(v2_lo - v2[0])
    e2_hi = jnp.exp(v2_hi - v2[0])
    colsum = lambda x: jnp.sum(x, axis=0, keepdims=True)
    cnt = [colsum(sel[0]) + colsum(sel[1])] + [colsum(sel[a + 1]) for a in range(1, hk)]
    mass = [colsum(sel[0] * e2_lo) + colsum(sel[1] * e2_hi)]
    mass += [colsum(sel[a + 1] * e2_lo) for a in range(1, hk)]
    z = jnp.zeros((1, n), F32)
    n1 = jnp.zeros(s1.shape, F32)
    for a in range(hk):
        z = z + jnp.exp(v1[a] - v1[0]) * mass[a]
        n1 = jnp.where(rank1 == float(a), cnt[a], n1)
    z = z + colsum(sel[-1] * jnp.exp(v1_hi - v1[0]))
    for a in range(hk, k):
        n1 = jnp.where(rank1 == float(a), sel[-1][a - hk:a - hk + 1, :], n1)
    e1 = jnp.exp(s1 - v1[0])
    e2 = jnp.exp(s2 - v2[0]) / z
    return n1, e1, rank2, e2


def _peer_topk_kernel(q_ref, key_ref, n1_ref, e1_ref, b2_ref, e2_ref, *, sub):
    tm = q_ref.shape[1]
    half = q_ref.shape[0] // 2
    for c in range(tm // sub):
        cs = slice(c * sub, (c + 1) * sub)
        s1 = jnp.dot(key_ref[0, 0], q_ref[:half, cs], preferred_element_type=F32)
        s2 = jnp.dot(key_ref[0, 1], q_ref[half:, cs], preferred_element_type=F32)
        n1, e1, b2, e2 = _peer_select(s1, s2)
        n1_ref[:, cs] = n1
        e1_ref[:, cs] = e1
        b2_ref[:, cs] = b2.astype(b2_ref.dtype)
        e2_ref[:, cs] = e2.astype(e2_ref.dtype)


def _peer_topk(qt, keys):
    t = qt.shape[1]
    heads = keys.shape[0]
    qd = qt.shape[0] // heads
    tm = min(256, t)
    out = pl.BlockSpec((PEER_N_KEYS, tm), lambda i, h: (h, i))
    shape = lambda dt: jax.ShapeDtypeStruct((heads * PEER_N_KEYS, t), dt)
    return pl.pallas_call(
        functools.partial(_peer_topk_kernel, sub=min(128, tm)),
        grid=(t // tm, heads),
        in_specs=[pl.BlockSpec((qd, tm), lambda i, h: (h, i)),
                  pl.BlockSpec((1, 2, PEER_N_KEYS, qd // 2), lambda i, h: (h, 0, 0, 0))],
        out_specs=[out, out, out, out],
        out_shape=[shape(F32), shape(F32), shape(BF16), shape(BF16)],
        compiler_params=_params(("arbitrary", "arbitrary")),
        name="peer_topk",
    )(qt, keys)


def _peer_dense_kernel(ht_ref, u_ref, v_ref, b2_ref, e2_ref, *rest, heads):
    n1_refs, e1_refs = rest[:heads], rest[heads:2 * heads]
    o_ref, a_ref = rest[2 * heads:]
    c = pl.program_id(1)
    tn = u_ref.shape[0]
    nk = PEER_N_KEYS

    @pl.when(c == 0)
    def _():
        o_ref[...] = jnp.zeros_like(o_ref)

    hid = jnp.dot(u_ref[...], ht_ref[...], preferred_element_type=F32)
    act = (0.5 * hid * (1.0 + lax.erf(hid * (2.0 ** -0.5)))).astype(BF16)
    for k in range(tn // nk):
        gate = None
        for h in range(heads):
            n1 = n1_refs[h][k:k + 1, :].astype(BF16)
            e1 = e1_refs[h][k:k + 1, :].astype(BF16)
            b2 = b2_ref[h * nk:(h + 1) * nk, :]
            e2 = e2_ref[h * nk:(h + 1) * nk, :]
            term = jnp.where(b2 < n1, e2, jnp.zeros_like(e2)) * e1
            gate = term if gate is None else gate + term
        a_ref[k * nk:(k + 1) * nk, :] = act[k * nk:(k + 1) * nk, :] * gate
    o_ref[...] += lax.dot_general(v_ref[...], a_ref[...], (((0,), (0,)), ((), ())),
                                  preferred_element_type=F32)


def _peer_dense(ht, u, v, n1, e1, b2, e2, heads):
    d, t = ht.shape
    n_exp = u.shape[0]
    tm = min(512, t)
    tn = min(1024, n_exp)
    rows = heads * PEER_N_KEYS
    per_chunk = tn // PEER_N_KEYS
    chunks_per_head = PEER_N_KEYS // per_chunk
    once = pl.Buffered(1)
    sel = pl.BlockSpec((rows, tm), lambda i, c: (0, i), pipeline_mode=once)
    head_rows = [pl.BlockSpec((per_chunk, tm), lambda i, c, h=h: (h * chunks_per_head + c, i))
                 for h in range(heads)]
    return pl.pallas_call(
        functools.partial(_peer_dense_kernel, heads=heads),
        grid=(t // tm, n_exp // tn),
        in_specs=[pl.BlockSpec((d, tm), lambda i, c: (0, i), pipeline_mode=once),
                  pl.BlockSpec((tn, d), lambda i, c: (c, 0)),
                  pl.BlockSpec((tn, d), lambda i, c: (c, 0)),
                  sel, sel] + head_rows + head_rows,
        out_specs=pl.BlockSpec((d, tm), lambda i, c: (0, i), pipeline_mode=once),
        out_shape=jax.ShapeDtypeStruct((d, t), F32),
        scratch_shapes=[pltpu.VMEM((tn, tm), BF16)],
        compiler_params=_params(("arbitrary", "arbitrary"), vmem=VMEM_LIMIT_LARGE),
        name="peer_dense_experts",
    )(ht, u, v, b2, e2, *([n1] * heads), *([e1] * heads))


def _rope_tables(seq):
    half = MLA_ROPE_DIM // 2
    pos = jnp.arange(seq, dtype=F32)
    inv_freq = ROPE_THETA ** (-jnp.arange(0, MLA_ROPE_DIM, 2, dtype=F32) / MLA_ROPE_DIM)
    ang = pos[:, None] * inv_freq[None, :]
    cos, sin = jnp.cos(ang), jnp.sin(ang)
    zero = jnp.zeros((seq, half), F32)
    pad = jnp.zeros((seq, LANES - MLA_ROPE_DIM), F32)
    cos_t = jnp.concatenate([cos, cos, pad], axis=1)
    sin_hi = jnp.concatenate([zero, sin, pad], axis=1)
    sin_lo = jnp.concatenate([-sin, zero, pad], axis=1)
    return cos_t, sin_hi, sin_lo


def _layer(x2, cond_in, batch, seq, w_ada, b_ada, g_attn_pre, g_attn_post, w_in, g_q_lat, w_q_b,
           g_kv_lat, w_kv_b, sinks, w_out, g_ffn_pre, g_ffn_post, w_peer_q, sub_keys, peer_u, peer_v):
    t, d = x2.shape
    q_rank = g_q_lat.shape[0]
    kv_rank = g_kv_lat.shape[0]
    n_mla = d // MLA_V_DIM
    kvw = SWA_KV_HEADS * SWA_HEAD_DIM
    heads = sub_keys.shape[0]

    ada = _ada(cond_in, w_ada, b_ada).reshape(batch, N_ADA, 1, d)
    shift_a, scale_a, gate_a, shift_f, scale_f, gate_f = [ada[:, i] for i in range(N_ADA)]

    w_a_cols = q_rank + kv_rank + MLA_ROPE_DIM
    w_a = jnp.pad(w_in[:, :w_a_cols], ((0, 0), (0, -w_a_cols % (2 * LANES)))).astype(BF16)
    off_swa = w_a_cols
    off_gate = off_swa + d + 2 * kvw
    w_swa = w_in[:, off_swa:off_gate].astype(BF16)
    w_g = w_in[:, off_gate:].astype(BF16)
    qk = MLA_NOPE_DIM + MLA_ROPE_DIM
    w_q = jnp.pad(w_q_b.reshape(q_rank, n_mla, qk),
                  ((0, 0), (0, 0), (0, MLA_QK_PAD - qk))).reshape(q_rank, n_mla * MLA_QK_PAD).astype(BF16)
    w_kv = w_kv_b.reshape(kv_rank, n_mla, MLA_NOPE_DIM + MLA_V_DIM)
    w_kn = w_kv[:, :, :MLA_NOPE_DIM].reshape(kv_rank, n_mla * MLA_NOPE_DIM).astype(BF16)
    w_v = w_kv[:, :, MLA_NOPE_DIM:].reshape(kv_rank, n_mla * MLA_V_DIM).astype(BF16)
    rope = _rope_tables(seq)

    h = _prenorm(x2, g_attn_pre, scale_a, shift_a, seq)
    z_a = _matmul(h, w_a, F32, tn=w_a.shape[1] // 2, name="in_proj_mla")
    z_swa = _matmul(h, w_swa, F32, tn=512, name="in_proj_swa")
    z_g = _matmul(h, w_g, F32, tn=512, name="in_proj_gates")
    q = _mla_q(z_a, g_q_lat, w_q, rope, seq, q_rank, n_mla)
    k, v = _mla_kv(z_a, g_kv_lat, w_kn, w_v, rope, seq, q_rank, kv_rank, n_mla)
    gated_a = _mla_attn(q, k, v, z_g, batch, seq, n_mla)
    mix = _swa(z_swa, z_g, gated_a, sinks, seq, d)
    y = _matmul(mix, w_out.astype(BF16), F32, tn=512, name="out_proj")
    x1, h2t = _post_attn(x2, y, g_attn_post, gate_a, g_ffn_pre, scale_f, shift_f, seq)

    pqt = _matmul(w_peer_q.T.astype(BF16), h2t, BF16, tn=512, name="peer_query")
    n1, e1, b2, e2 = _peer_topk(pqt, sub_keys.astype(BF16))
    yt = _peer_dense(h2t, peer_u.astype(BF16), peer_v.astype(BF16), n1, e1, b2, e2, heads)
    return _final(x1, yt, g_ffn_post, gate_f, seq)


def kernel(x, c, w_ada, b_ada, g_attn_pre, g_attn_post, w_in, g_q_lat, w_q_b, g_kv_lat, w_kv_b, sinks, w_out, g_ffn_pre, g_ffn_post, w_peer_q, peer_sub_keys, peer_u, peer_v):
    batch, seq, d = x.shape
    x2 = x.reshape(batch * seq, d)
    for l in range(w_ada.shape[0]):
        x2 = _layer(x2, c, batch, seq, w_ada[l], b_ada[l], g_attn_pre[l], g_attn_post[l], w_in[l],
                    g_q_lat[l], w_q_b[l], g_kv_lat[l], w_kv_b[l], sinks[l], w_out[l], g_ffn_pre[l],
                    g_ffn_post[l], w_peer_q[l], peer_sub_keys[l], peer_u[l], peer_v[l])
    return x2.reshape(batch, seq, d)
```

```python
import functools
import math

import jax
import jax.numpy as jnp
from jax import lax
from jax.experimental import pallas as pl
from jax.experimental.pallas import tpu as pltpu

F32 = jnp.float32
BF16 = jnp.bfloat16

MLA_NOPE_DIM = 128
MLA_ROPE_DIM = 64
MLA_V_DIM = 128
MLA_QK_PAD = 256
ROPE_THETA = 10000.0
SWA_HEAD_DIM = 64
SWA_KV_HEADS = 8
WINDOW = 128
PEER_HEADS = 8
PEER_N_KEYS = 128
PEER_TOPK = 16
N_ADA = 6
NORM_EPS = 1e-6
NEG_INF = -1e30

LANES = 128
VMEM_LIMIT = 56 * 1024 * 1024
VMEM_LIMIT_LARGE = 60 * 1024 * 1024


def _params(sem, vmem=VMEM_LIMIT):
    return pltpu.CompilerParams(dimension_semantics=sem, vmem_limit_bytes=vmem)


def _rms(x, g):
    return x * lax.rsqrt(jnp.mean(x * x, axis=-1, keepdims=True) + NORM_EPS) * g


def _ada_kernel(cb_ref, w_ref, b_ref, o_ref):
    nb = cb_ref.shape[0]
    tn = w_ref.shape[1]
    for b in range(nb):
        cb = cb_ref[b]
        cond = cb * jax.nn.sigmoid(cb)
        for j in range(tn // LANES):
            sl = slice(j * LANES, (j + 1) * LANES)
            r = jnp.sum(w_ref[:, sl] * cond, axis=0, keepdims=True)
            o_ref[b:b + 1, sl] = r + b_ref[:, sl]


def _ada(c, w, bias):
    nb, k = c.shape
    n = w.shape[1]
    tn = min(512, n)
    cb = jnp.broadcast_to(c[:, :, None], (nb, k, LANES))
    return pl.pallas_call(
        _ada_kernel,
        grid=(n // tn,),
        in_specs=[pl.BlockSpec((nb, k, LANES), lambda j: (0, 0, 0)),
                  pl.BlockSpec((k, tn), lambda j: (0, j)),
                  pl.BlockSpec((1, tn), lambda j: (0, j))],
        out_specs=pl.BlockSpec((nb, tn), lambda j: (0, j)),
        out_shape=jax.ShapeDtypeStruct((nb, n), F32),
        compiler_params=_params(("arbitrary",)),
        name="ada_matvec",
    )(cb, w, bias.reshape(1, n))


def _prenorm_kernel(x_ref, g_ref, sc_ref, sh_ref, o_ref):
    h = _rms(x_ref[...], g_ref[...]) * (1.0 + sc_ref[0]) + sh_ref[0]
    o_ref[...] = h.astype(o_ref.dtype)


def _prenorm(x2, g, scale, shift, seq):
    t, d = x2.shape
    tm = min(256, seq)
    per = seq // tm
    row = pl.BlockSpec((tm, d), lambda i: (i, 0))
    mod = pl.BlockSpec((1, 1, d), lambda i: (i // per, 0, 0))
    return pl.pallas_call(
        _prenorm_kernel,
        grid=(t // tm,),
        in_specs=[row, pl.BlockSpec((1, d), lambda i: (0, 0)), mod, mod],
        out_specs=row,
        out_shape=jax.ShapeDtypeStruct((t, d), BF16),
        compiler_params=_params(("arbitrary",)),
        name="prenorm_modulate",
    )(x2, g.reshape(1, d), scale, shift)


def _post_kernel(x_ref, y_ref, gpost_ref, gate_ref, gpre_ref, sc_ref, sh_ref, x1_ref, h_ref):
    x1 = x_ref[...] + gate_ref[0] * _rms(y_ref[...], gpost_ref[...])
    x1_ref[...] = x1
    h = _rms(x1, gpre_ref[...]) * (1.0 + sc_ref[0]) + sh_ref[0]
    h_ref[...] = h.T.astype(h_ref.dtype)


def _post_attn(x2, y, g_post, gate, g_pre, scale, shift, seq):
    t, d = x2.shape
    tm = min(256, seq)
    per = seq // tm
    row = pl.BlockSpec((tm, d), lambda i: (i, 0))
    vec = pl.BlockSpec((1, d), lambda i: (0, 0))
    mod = pl.BlockSpec((1, 1, d), lambda i: (i // per, 0, 0))
    return pl.pallas_call(
        _post_kernel,
        grid=(t // tm,),
        in_specs=[row, row, vec, mod, vec, mod, mod],
        out_specs=[row, pl.BlockSpec((d, tm), lambda i: (0, i))],
        out_shape=[jax.ShapeDtypeStruct((t, d), F32), jax.ShapeDtypeStruct((d, t), BF16)],
        compiler_params=_params(("arbitrary",)),
        name="post_attn_norm",
    )(x2, y, g_post.reshape(1, d), gate, g_pre.reshape(1, d), scale, shift)


def _final_kernel(x_ref, yt_ref, g_ref, gate_ref, o_ref):
    o_ref[...] = x_ref[...] + gate_ref[0] * _rms(yt_ref[...].T, g_ref[...])


def _final(x1, yt, g, gate, seq):
    t, d = x1.shape
    tm = min(256, seq)
    per = seq // tm
    row = pl.BlockSpec((tm, d), lambda i: (i, 0))
    return pl.pallas_call(
        _final_kernel,
        grid=(t // tm,),
        in_specs=[row, pl.BlockSpec((d, tm), lambda i: (0, i)), pl.BlockSpec((1, d), lambda i: (0, 0)),
                  pl.BlockSpec((1, 1, d), lambda i: (i // per, 0, 0))],
        out_specs=row,
        out_shape=jax.ShapeDtypeStruct((t, d), F32),
        compiler_params=_params(("arbitrary",)),
        name="final_residual",
    )(x1, yt, g.reshape(1, d), gate)


def _mm_kernel(a_ref, w_ref, o_ref):
    o_ref[...] = jnp.dot(a_ref[...], w_ref[...], preferred_element_type=F32).astype(o_ref.dtype)


def _matmul(a, w, out_dtype, tn, tm=1024, name="matmul"):
    m, k = a.shape
    n = w.shape[1]
    tm = min(tm, m)
    tn = min(tn, n)
    return pl.pallas_call(
        _mm_kernel,
        grid=(m // tm, n // tn),
        in_specs=[pl.BlockSpec((tm, k), lambda i, j: (i, 0)),
                  pl.BlockSpec((k, tn), lambda i, j: (0, j))],
        out_specs=pl.BlockSpec((tm, tn), lambda i, j: (i, j)),
        out_shape=jax.ShapeDtypeStruct((m, n), out_dtype),
        compiler_params=_params(("arbitrary", "arbitrary")),
        name=name,
    )(a, w)


def _rope_upper(hi, cos_ref, sin_hi_ref, sin_lo_ref):
    return (hi * cos_ref[...]
            + pltpu.roll(hi, 32, axis=1) * sin_hi_ref[...]
            + pltpu.roll(hi, 96, axis=1) * sin_lo_ref[...])


def _mla_q_kernel(z_ref, g_ref, w_ref, cos_ref, s1_ref, s2_ref, o_ref, *, scale, heads):
    qn = _rms(z_ref[...], g_ref[...]).astype(BF16)
    y = jnp.dot(qn, w_ref[...], preferred_element_type=F32)
    for h in range(heads):
        lo = y[:, h * MLA_QK_PAD:h * MLA_QK_PAD + LANES]
        hi = y[:, h * MLA_QK_PAD + LANES:(h + 1) * MLA_QK_PAD]
        o_ref[h * MLA_QK_PAD:h * MLA_QK_PAD + LANES, :] = (lo * scale).T.astype(o_ref.dtype)
        o_ref[h * MLA_QK_PAD + LANES:(h + 1) * MLA_QK_PAD, :] = (
            _rope_upper(hi, cos_ref, s1_ref, s2_ref) * scale).T.astype(o_ref.dtype)


def _mla_q(z_a, g_q, w_q, rope, seq, q_rank, n_heads):
    t = z_a.shape[0]
    tm = min(512, seq)
    hb = min(4, n_heads)
    per = seq // tm
    scale = (MLA_NOPE_DIM + MLA_ROPE_DIM) ** -0.5 * math.log2(math.e)
    tab = pl.BlockSpec((tm, LANES), lambda i, j: (i % per, 0))
    return pl.pallas_call(
        functools.partial(_mla_q_kernel, scale=scale, heads=hb),
        grid=(t // tm, n_heads // hb),
        in_specs=[pl.BlockSpec((tm, q_rank), lambda i, j: (i, 0)),
                  pl.BlockSpec((1, q_rank), lambda i, j: (0, 0)),
                  pl.BlockSpec((q_rank, hb * MLA_QK_PAD), lambda i, j: (0, j)),
                  tab, tab, tab],
        out_specs=pl.BlockSpec((hb * MLA_QK_PAD, tm), lambda i, j: (j, i)),
        out_shape=jax.ShapeDtypeStruct((n_heads * MLA_QK_PAD, t), BF16),
        compiler_params=_params(("arbitrary", "arbitrary")),
        name="mla_q_proj",
    )(z_a, g_q.reshape(1, q_rank), w_q, *rope)


def _mla_kv_kernel(c_ref, pe_ref, g_ref, wk_ref, wv_ref, cos_ref, s1_ref, s2_ref,
                   k_ref, v_ref, *, heads):
    c = _rms(c_ref[...], g_ref[...]).astype(BF16)
    kn = jnp.dot(c, wk_ref[...], preferred_element_type=F32)
    v_ref[...] = jnp.dot(c, wv_ref[...], preferred_element_type=F32).T.astype(v_ref.dtype)
    pe = _rope_upper(pe_ref[...], cos_ref, s1_ref, s2_ref).astype(k_ref.dtype)
    for h in range(heads):
        k_ref[:, h * MLA_QK_PAD:h * MLA_QK_PAD + LANES] = (
            kn[:, h * LANES:(h + 1) * LANES].astype(k_ref.dtype))
        k_ref[:, h * MLA_QK_PAD + LANES:(h + 1) * MLA_QK_PAD] = pe


def _mla_kv(z_a, g_kv, w_kn, w_v, rope, seq, q_rank, kv_rank, n_heads):
    t = z_a.shape[0]
    tm = min(256, seq)
    per = seq // tm
    tk = _attn_block(seq)
    sub = tk // tm
    tab = pl.BlockSpec((tm, LANES), lambda i: (i % per, 0))
    full = lambda shape: pl.BlockSpec(shape, lambda i: (0, 0))
    return pl.pallas_call(
        functools.partial(_mla_kv_kernel, heads=n_heads),
        grid=(t // tm,),
        in_specs=[pl.BlockSpec((tm, kv_rank), lambda i: (i, q_rank // kv_rank)),
                  pl.BlockSpec((tm, LANES), lambda i: (i, (q_rank + kv_rank) // LANES)),
                  full((1, kv_rank)),
                  full((kv_rank, n_heads * MLA_NOPE_DIM)),
                  full((kv_rank, n_heads * MLA_V_DIM)),
                  tab, tab, tab],
        out_specs=[pl.BlockSpec((tm, n_heads * MLA_QK_PAD), lambda i: (i, 0)),
                   pl.BlockSpec((None, n_heads * MLA_V_DIM, tm), lambda i: (i // sub, 0, i % sub))],
        out_shape=[jax.ShapeDtypeStruct((t, n_heads * MLA_QK_PAD), BF16),
                   jax.ShapeDtypeStruct((t // tk, n_heads * MLA_V_DIM, tk), BF16)],
        compiler_params=_params(("arbitrary",)),
        name="mla_kv_proj",
    )(z_a, z_a, g_kv.reshape(1, kv_rank), w_kn, w_v, *rope)


def _attn_block(seq):
    return min(1024, seq)


def _mla_attn_kernel(qt_ref, k_ref, vt_ref, za_ref, o_ref, s_ref, m_ref, l_ref, acc_ref, *, tq):
    qi = pl.program_id(2)

    def put_scores(slot, j):
        kb = k_ref[pl.ds(pl.multiple_of(j * tq, tq), tq), :]
        s_ref[slot] = jnp.dot(kb, qt_ref[...], preferred_element_type=F32)

    def update(s, j):
        m = m_ref[...]
        m_new = jnp.maximum(m, jnp.max(s, axis=0, keepdims=True))
        alpha = jnp.exp2(m - m_new)
        p = jnp.exp2(s - m_new)
        acc_ref[...] = alpha * acc_ref[...] + jnp.dot(vt_ref[j], p.astype(BF16),
                                                      preferred_element_type=F32)
        l_ref[...] = alpha * l_ref[...] + jnp.sum(p, axis=0, keepdims=True)
        m_ref[...] = m_new

    def diagonal(slot):
        key = lax.broadcasted_iota(jnp.int32, (tq, tq), 0)
        qry = lax.broadcasted_iota(jnp.int32, (tq, tq), 1)
        update(jnp.where(key <= qry, s_ref[slot], NEG_INF), qi)

    m_ref[...] = jnp.full(m_ref.shape, NEG_INF, F32)
    l_ref[...] = jnp.zeros(l_ref.shape, F32)
    acc_ref[...] = jnp.zeros(acc_ref.shape, F32)
    put_scores(0, 0)

    def pair(jj, _):
        j = 2 * jj
        put_scores(1, j + 1)
        update(s_ref[0], j)
        put_scores(0, j + 2)
        update(s_ref[1], j + 1)
        return 0

    lax.fori_loop(0, qi // 2, pair, 0)

    @pl.when(qi % 2 == 1)
    def _():
        put_scores(1, qi)
        update(s_ref[0], qi - 1)
        diagonal(1)

    @pl.when(qi % 2 == 0)
    def _():
        diagonal(0)

    o = (acc_ref[...] / l_ref[...]).T
    o_ref[...] = (o * jax.nn.sigmoid(za_ref[...])).astype(o_ref.dtype)


def _mla_attn(qt, k, vt, z_g, batch, seq, n_heads):
    t = k.shape[0]
    tq = _attn_block(seq)
    nq = seq // tq
    return pl.pallas_call(
        functools.partial(_mla_attn_kernel, tq=tq),
        grid=(batch, n_heads, nq),
        in_specs=[pl.BlockSpec((MLA_QK_PAD, tq), lambda b, h, i: (h, b * nq + i)),
                  pl.BlockSpec((seq, MLA_QK_PAD), lambda b, h, i: (b, h)),
                  pl.BlockSpec((nq, MLA_V_DIM, tq), lambda b, h, i: (b, h, 0)),
                  pl.BlockSpec((tq, MLA_V_DIM), lambda b, h, i: (b * nq + i, h))],
        out_specs=pl.BlockSpec((tq, MLA_V_DIM), lambda b, h, i: (b * nq + i, h)),
        out_shape=jax.ShapeDtypeStruct((t, n_heads * MLA_V_DIM), BF16),
        scratch_shapes=[pltpu.VMEM((2, tq, tq), F32), pltpu.VMEM((1, tq), F32),
                        pltpu.VMEM((1, tq), F32), pltpu.VMEM((MLA_V_DIM, tq), F32)],
        compiler_params=_params(("arbitrary", "arbitrary", "arbitrary")),
        name="mla_flash_attention",
    )(qt, k, vt, z_g)


def _swa_kernel(sink_ref, q_ref, kp_ref, kc_ref, vp_ref, vc_ref, zb_ref, a_ref, o_ref,
                *, per, n_heads):
    i = pl.program_id(0)
    has_prev = (i % per) > 0
    group = n_heads // SWA_KV_HEADS
    hd = SWA_HEAD_DIM
    log2e = math.log2(math.e)
    scale = hd ** -0.5 * log2e
    span = 2 * WINDOW
    key = lax.broadcasted_iota(jnp.int32, (span, WINDOW), 0)
    qry = lax.broadcasted_iota(jnp.int32, (span, WINDOW), 1)
    dist = qry + WINDOW - key
    valid = (dist >= 0) & (dist < WINDOW) & ((key >= WINDOW) | has_prev)
    distf = dist.astype(F32)
    nt = (((1,), (1,)), ((), ()))
    tn = (((0,), (0,)), ((), ()))
    for n in range(SWA_KV_HEADS):
        ksl = slice(n * hd, (n + 1) * hd)
        kb = jnp.concatenate([kp_ref[:, ksl], kc_ref[:, ksl]], axis=0).astype(BF16)
        vb = jnp.concatenate([vp_ref[:, ksl], vc_ref[:, ksl]], axis=0).astype(BF16)
        heads = range(n * group, (n + 1) * group)
        qs = [(q_ref[:, hq * hd:(hq + 1) * hd] * scale).astype(BF16) for hq in heads]
        ss = [lax.dot_general(kb, qh, nt, preferred_element_type=F32) for qh in qs]
        ps, dens = [], []
        for hq, s in zip(heads, ss):
            slope = 2.0 ** (-8.0 * (hq + 1) / n_heads) * log2e
            sink = sink_ref[hq] * log2e
            s = jnp.where(valid, s - slope * distf, NEG_INF)
            m = jnp.maximum(jnp.max(s, axis=0, keepdims=True), sink)
            p = jnp.exp2(s - m)
            dens.append(jnp.sum(p, axis=0, keepdims=True) + jnp.exp2(sink - m))
            ps.append(p.astype(BF16))
        outs = [lax.dot_general(vb, p, tn, preferred_element_type=F32) / den
                for p, den in zip(ps, dens)]
        for g in range(0, group, 2):
            hq = n * group + g
            sl = slice(hq * hd, (hq + 2) * hd)
            o2 = jnp.concatenate([outs[g], outs[g + 1]], axis=0).T
            mix = a_ref[:, sl].astype(F32) + jax.nn.sigmoid(zb_ref[:, sl]) * o2
            o_ref[:, sl] = mix.astype(o_ref.dtype)


def _swa(z_swa, z_g, gated_a, sinks, seq, d):
    t = z_swa.shape[0]
    n_heads = d // SWA_HEAD_DIM
    kvw = SWA_KV_HEADS * SWA_HEAD_DIM
    per = seq // WINDOW
    kcol = d // kvw
    prev = lambda c: (lambda i: (jnp.maximum(i - 1, 0), c))
    cur = lambda c: (lambda i: (i, c))
    return pl.pallas_call(
        functools.partial(_swa_kernel, per=per, n_heads=n_heads),
        grid=(t // WINDOW,),
        in_specs=[pl.BlockSpec(memory_space=pltpu.SMEM),
                  pl.BlockSpec((WINDOW, d), cur(0)),
                  pl.BlockSpec((WINDOW, kvw), prev(kcol)),
                  pl.BlockSpec((WINDOW, kvw), cur(kcol)),
                  pl.BlockSpec((WINDOW, kvw), prev(kcol + 1)),
                  pl.BlockSpec((WINDOW, kvw), cur(kcol + 1)),
                  pl.BlockSpec((WINDOW, d), cur(1)),
                  pl.BlockSpec((WINDOW, d), cur(0))],
        out_specs=pl.BlockSpec((WINDOW, d), cur(0)),
        out_shape=jax.ShapeDtypeStruct((t, d), BF16),
        compiler_params=_params(("arbitrary",)),
        name="swa_sink_attention",
    )(sinks, z_swa, z_swa, z_swa, z_swa, z_swa, z_g, gated_a)


def _top16(ss, exact):
    ss = list(ss)
    rows = lax.broadcasted_iota(jnp.int32, ss[0].shape, 0).astype(F32)
    ranks = [jnp.full(s.shape, float(PEER_TOPK), F32) for s in ss]
    vals = [[] for _ in ss]
    for r in range(PEER_TOPK):
        for i, s in enumerate(ss):
            m = jnp.max(s, axis=0, keepdims=True)
            if exact:
                first = jnp.min(jnp.where(s == m, rows, float(PEER_N_KEYS)), axis=0, keepdims=True)
                hit = rows == first
            else:
                hit = s == m
            ranks[i] = jnp.where(hit, float(r), ranks[i])
            ss[i] = jnp.where(hit, NEG_INF, s)
            vals[i].append(m)
    return list(zip(vals, ranks))


def _peer_select(s1, s2, exact):
    k = PEER_TOPK
    (v1, rank1), (v2, rank2) = _top16([s1, s2], exact)
    n = s1.shape[1]
    hk = k // 2
    r8 = lax.broadcasted_iota(jnp.int32, (hk, n), 0).astype(F32)
    colsum = lambda x: jnp.sum(x, axis=0, keepdims=True)

    def stack(vals):
        out = jnp.zeros((hk, n), F32)
        for i, v in enumerate(vals):
            out = jnp.where(r8 == float(i), v, out)
        return out

    v2_lo, v2_hi, v1_hi = stack(v2[:hk]), stack(v2[hk:]), stack(v1[hk:])
    cand = [v1[0] + v2_lo, v1[0] + v2_hi]
    code = [r8, r8 + float(hk)]
    for a in range(1, hk):
        cand.append(jnp.where(r8 < float(k // (a + 1)), v1[a] + v2_lo, NEG_INF))
        code.append(r8 + float(a * k))
    cand.append(v1_hi + v2[0])
    code.append((r8 + float(hk)) * float(k))
    sel = [jnp.zeros((hk, n), F32) for _ in cand]
    for _ in range(k):
        m = jnp.max(functools.reduce(jnp.maximum, cand), axis=0, keepdims=True)
        if exact:
            first = functools.reduce(
                jnp.minimum, [jnp.where(cv == m, cd, float(k * k)) for cv, cd in zip(cand, code)])
            first = jnp.min(first, axis=0, keepdims=True)
        for i in range(len(cand)):
            hit = (code[i] == first) if exact else (cand[i] == m)
            sel[i] = jnp.where(hit, 1.0, sel[i])
            cand[i] = jnp.where(hit, NEG_INF, cand[i])
    e2_lo = jnp.exp(v2_lo - v2[0])
    e2_hi = jnp.exp(v2_hi - v2[0])
    cnt = [colsum(sel[0]) + colsum(sel[1])] + [colsum(sel[a + 1]) for a in range(1, hk)]
    mass = [colsum(sel[0] * e2_lo) + colsum(sel[1] * e2_hi)]
    mass += [colsum(sel[a + 1] * e2_lo) for a in range(1, hk)]
    z = jnp.zeros((1, n), F32)
    n1 = jnp.zeros(s1.shape, F32)
    for a in range(hk):
        z = z + jnp.exp(v1[a] - v1[0]) * mass[a]
        n1 = jnp.where(rank1 == float(a), cnt[a], n1)
    z = z + colsum(sel[-1] * jnp.exp(v1_hi - v1[0]))
    for a in range(hk, k):
        n1 = jnp.where(rank1 == float(a), sel[-1][a - hk:a - hk + 1, :], n1)
    e1 = jnp.exp(s1 - v1[0])
    e2 = jnp.exp(s2 - v2[0]) / z
    if exact:
        bad = jnp.zeros((1, n), F32)
    else:
        ranked = lambda rk: colsum(jnp.where(rk < float(k), 1.0, 0.0))
        picked = functools.reduce(lambda x, y: x + y, cnt) + colsum(sel[-1])
        bad = (jnp.abs(ranked(rank1) - float(k)) + jnp.abs(ranked(rank2) - float(k))
               + jnp.abs(picked - float(k)))
    return (n1, e1, rank2, e2), bad


def _peer_topk_kernel(q_ref, key_ref, n1_ref, e1_ref, b2_ref, e2_ref, *, sub):
    tm = q_ref.shape[1]
    half = q_ref.shape[0] // 2

    def scores(cs):
        s1 = jnp.dot(key_ref[0, 0], q_ref[:half, cs], preferred_element_type=F32)
        s2 = jnp.dot(key_ref[0, 1], q_ref[half:, cs], preferred_element_type=F32)
        return s1, s2

    def store(cs, res):
        n1, e1, b2, e2 = res
        n1_ref[:, cs] = n1
        e1_ref[:, cs] = e1
        b2_ref[:, cs] = b2.astype(b2_ref.dtype)
        e2_ref[:, cs] = e2.astype(e2_ref.dtype)

    for c in range(tm // sub):
        cs = slice(c * sub, (c + 1) * sub)
        res, bad = _peer_select(*scores(cs), exact=False)
        store(cs, res)

        @pl.when(jnp.max(bad) > 0.0)
        def _():
            store(cs, _peer_select(*scores(cs), exact=True)[0])


def _peer_topk(qt, keys):
    t = qt.shape[1]
    heads = keys.shape[0]
    qd = qt.shape[0] // heads
    tm = min(256, t)
    out = pl.BlockSpec((PEER_N_KEYS, tm), lambda i, h: (h, i))
    shape = lambda dt: jax.ShapeDtypeStruct((heads * PEER_N_KEYS, t), dt)
    return pl.pallas_call(
        functools.partial(_peer_topk_kernel, sub=min(128, tm)),
        grid=(t // tm, heads),
        in_specs=[pl.BlockSpec((qd, tm), lambda i, h: (h, i)),
                  pl.BlockSpec((1, 2, PEER_N_KEYS, qd // 2), lambda i, h: (h, 0, 0, 0))],
        out_specs=[out, out, out, out],
        out_shape=[shape(F32), shape(F32), shape(BF16), shape(BF16)],
        compiler_params=_params(("arbitrary", "arbitrary")),
        name="peer_topk",
    )(qt, keys)


def _peer_dense_kernel(ht_ref, u_ref, v_ref, b2_ref, e2_ref, *rest, heads):
    n1_refs, e1_refs = rest[:heads], rest[heads:2 * heads]
    o_ref, a_ref = rest[2 * heads:]
    c = pl.program_id(1)
    tn = u_ref.shape[0]
    nk = PEER_N_KEYS

    @pl.when(c == 0)
    def _():
        o_ref[...] = jnp.zeros_like(o_ref)

    hid = jnp.dot(u_ref[...], ht_ref[...], preferred_element_type=F32)
    act = (0.5 * hid * (1.0 + lax.erf(hid * (2.0 ** -0.5)))).astype(BF16)
    for k in range(tn // nk):
        gate = None
        for h in range(heads):
            n1 = n1_refs[h][k:k + 1, :].astype(BF16)
            e1 = e1_refs[h][k:k + 1, :].astype(BF16)
            b2 = b2_ref[h * nk:(h + 1) * nk, :]
            e2 = e2_ref[h * nk:(h + 1) * nk, :]
            term = jnp.where(b2 < n1, e2, jnp.zeros_like(e2)) * e1
            gate = term if gate is None else gate + term
        a_ref[k * nk:(k + 1) * nk, :] = act[k * nk:(k + 1) * nk, :] * gate
    o_ref[...] += lax.dot_general(v_ref[...], a_ref[...], (((0,), (0,)), ((), ())),
                                  preferred_element_type=F32)


def _peer_dense(ht, u, v, n1, e1, b2, e2, heads):
    d, t = ht.shape
    n_exp = u.shape[0]
    tm = min(512, t)
    tn = min(1024, n_exp)
    rows = heads * PEER_N_KEYS
    per_chunk = tn // PEER_N_KEYS
    chunks_per_head = PEER_N_KEYS // per_chunk
    once = pl.Buffered(1)
    sel = pl.BlockSpec((rows, tm), lambda i, c: (0, i), pipeline_mode=once)
    head_rows = [pl.BlockSpec((per_chunk, tm), lambda i, c, h=h: (h * chunks_per_head + c, i))
                 for h in range(heads)]
    return pl.pallas_call(
        functools.partial(_peer_dense_kernel, heads=heads),
        grid=(t // tm, n_exp // tn),
        in_specs=[pl.BlockSpec((d, tm), lambda i, c: (0, i), pipeline_mode=once),
                  pl.BlockSpec((tn, d), lambda i, c: (c, 0)),
                  pl.BlockSpec((tn, d), lambda i, c: (c, 0)),
                  sel, sel] + head_rows + head_rows,
        out_specs=pl.BlockSpec((d, tm), lambda i, c: (0, i), pipeline_mode=once),
        out_shape=jax.ShapeDtypeStruct((d, t), F32),
        scratch_shapes=[pltpu.VMEM((tn, tm), BF16)],
        compiler_params=_params(("arbitrary", "arbitrary"), vmem=VMEM_LIMIT_LARGE),
        name="peer_dense_experts",
    )(ht, u, v, b2, e2, *([n1] * heads), *([e1] * heads))


def _rope_tables(seq):
    half = MLA_ROPE_DIM // 2
    pos = jnp.arange(seq, dtype=F32)
    inv_freq = ROPE_THETA ** (-jnp.arange(0, MLA_ROPE_DIM, 2, dtype=F32) / MLA_ROPE_DIM)
    ang = pos[:, None] * inv_freq[None, :]
    cos, sin = jnp.cos(ang), jnp.sin(ang)
    zero = jnp.zeros((seq, half), F32)
    pad = jnp.zeros((seq, LANES - MLA_ROPE_DIM), F32)
    cos_t = jnp.concatenate([cos, cos, pad], axis=1)
    sin_hi = jnp.concatenate([zero, sin, pad], axis=1)
    sin_lo = jnp.concatenate([-sin, zero, pad], axis=1)
    return cos_t, sin_hi, sin_lo


def _layer(x2, cond_in, batch, seq, w_ada, b_ada, g_attn_pre, g_attn_post, w_in, g_q_lat, w_q_b,
           g_kv_lat, w_kv_b, sinks, w_out, g_ffn_pre, g_ffn_post, w_peer_q, sub_keys, peer_u, peer_v):
    t, d = x2.shape
    q_rank = g_q_lat.shape[0]
    kv_rank = g_kv_lat.shape[0]
    n_mla = d // MLA_V_DIM
    kvw = SWA_KV_HEADS * SWA_HEAD_DIM
    heads = sub_keys.shape[0]

    ada = _ada(cond_in, w_ada, b_ada).reshape(batch, N_ADA, 1, d)
    shift_a, scale_a, gate_a, shift_f, scale_f, gate_f = [ada[:, i] for i in range(N_ADA)]

    w_a_cols = q_rank + kv_rank + MLA_ROPE_DIM
    w_a = jnp.pad(w_in[:, :w_a_cols], ((0, 0), (0, -w_a_cols % (2 * LANES)))).astype(BF16)
    off_swa = w_a_cols
    off_gate = off_swa + d + 2 * kvw
    w_swa = w_in[:, off_swa:off_gate].astype(BF16)
    w_g = w_in[:, off_gate:].astype(BF16)
    qk = MLA_NOPE_DIM + MLA_ROPE_DIM
    w_q = jnp.pad(w_q_b.reshape(q_rank, n_mla, qk),
                  ((0, 0), (0, 0), (0, MLA_QK_PAD - qk))).reshape(q_rank, n_mla * MLA_QK_PAD).astype(BF16)
    w_kv = w_kv_b.reshape(kv_rank, n_mla, MLA_NOPE_DIM + MLA_V_DIM)
    w_kn = w_kv[:, :, :MLA_NOPE_DIM].reshape(kv_rank, n_mla * MLA_NOPE_DIM).astype(BF16)
    w_v = w_kv[:, :, MLA_NOPE_DIM:].reshape(kv_rank, n_mla * MLA_V_DIM).astype(BF16)
    rope = _rope_tables(seq)

    h = _prenorm(x2, g_attn_pre, scale_a, shift_a, seq)
    z_a = _matmul(h, w_a, F32, tn=w_a.shape[1] // 2, name="in_proj_mla")
    z_swa = _matmul(h, w_swa, F32, tn=512, name="in_proj_swa")
    z_g = _matmul(h, w_g, F32, tn=512, name="in_proj_gates")
    q = _mla_q(z_a, g_q_lat, w_q, rope, seq, q_rank, n_mla)
    k, v = _mla_kv(z_a, g_kv_lat, w_kn, w_v, rope, seq, q_rank, kv_rank, n_mla)
    gated_a = _mla_attn(q, k, v, z_g, batch, seq, n_mla)
    mix = _swa(z_swa, z_g, gated_a, sinks, seq, d)
    y = _matmul(mix, w_out.astype(BF16), F32, tn=512, name="out_proj")
    x1, h2t = _post_attn(x2, y, g_attn_post, gate_a, g_ffn_pre, scale_f, shift_f, seq)

    pqt = _matmul(w_peer_q.T.astype(BF16), h2t, BF16, tn=512, name="peer_query")
    n1, e1, b2, e2 = _peer_topk(pqt, sub_keys.astype(BF16))
    yt = _peer_dense(h2t, peer_u.astype(BF16), peer_v.astype(BF16), n1, e1, b2, e2, heads)
    return _final(x1, yt, g_ffn_post, gate_f, seq)


def kernel(x, c, w_ada, b_ada, g_attn_pre, g_attn_post, w_in, g_q_lat, w_q_b, g_kv_lat, w_kv_b, sinks, w_out, g_ffn_pre, g_ffn_post, w_peer_q, peer_sub_keys, peer_u, peer_v):
    batch, seq, d = x.shape
    x2 = x.reshape(batch * seq, d)
    for l in range(w_ada.shape[0]):
        x2 = _layer(x2, c, batch, seq, w_ada[l], b_ada[l], g_attn_pre[l], g_attn_post[l], w_in[l],
                    g_q_lat[l], w_q_b[l], g_kv_lat[l], w_kv_b[l], sinks[l], w_out[l], g_ffn_pre[l],
                    g_ffn_post[l], w_peer_q[l], peer_sub_keys[l], peer_u[l], peer_v[l])
    return x2.reshape(batch, seq, d)
```

```python
import functools
import math

import jax
import jax.numpy as jnp
from jax import lax
from jax.experimental import pallas as pl
from jax.experimental.pallas import tpu as pltpu

F32 = jnp.float32
BF16 = jnp.bfloat16

MLA_NOPE_DIM = 128
MLA_ROPE_DIM = 64
MLA_V_DIM = 128
MLA_QK_PAD = 256
ROPE_THETA = 10000.0
SWA_HEAD_DIM = 64
SWA_KV_HEADS = 8
WINDOW = 128
PEER_HEADS = 8
PEER_N_KEYS = 128
PEER_TOPK = 16
N_ADA = 6
NORM_EPS = 1e-6
NEG_INF = -1e30

LANES = 128
VMEM_LIMIT = 56 * 1024 * 1024
VMEM_LIMIT_LARGE = 60 * 1024 * 1024


def _params(sem, vmem=VMEM_LIMIT):
    return pltpu.CompilerParams(dimension_semantics=sem, vmem_limit_bytes=vmem)


def _rms(x, g):
    return x * lax.rsqrt(jnp.mean(x * x, axis=-1, keepdims=True) + NORM_EPS) * g


def _ada_kernel(cb_ref, w_ref, b_ref, o_ref):
    nb = cb_ref.shape[0]
    tn = w_ref.shape[1]
    for b in range(nb):
        cb = cb_ref[b]
        cond = cb * jax.nn.sigmoid(cb)
        for j in range(tn // LANES):
            sl = slice(j * LANES, (j + 1) * LANES)
            r = jnp.sum(w_ref[:, sl] * cond, axis=0, keepdims=True)
            o_ref[b:b + 1, sl] = r + b_ref[:, sl]


def _ada(c, w, bias):
    nb, k = c.shape
    n = w.shape[1]
    tn = min(512, n)
    cb = jnp.broadcast_to(c[:, :, None], (nb, k, LANES))
    return pl.pallas_call(
        _ada_kernel,
        grid=(n // tn,),
        in_specs=[pl.BlockSpec((nb, k, LANES), lambda j: (0, 0, 0)),
                  pl.BlockSpec((k, tn), lambda j: (0, j)),
                  pl.BlockSpec((1, tn), lambda j: (0, j))],
        out_specs=pl.BlockSpec((nb, tn), lambda j: (0, j)),
        out_shape=jax.ShapeDtypeStruct((nb, n), F32),
        compiler_params=_params(("arbitrary",)),
        name="ada_matvec",
    )(cb, w, bias.reshape(1, n))


def _prenorm_kernel(x_ref, g_ref, sc_ref, sh_ref, o_ref):
    h = _rms(x_ref[...], g_ref[...]) * (1.0 + sc_ref[0]) + sh_ref[0]
    o_ref[...] = h.astype(o_ref.dtype)


def _prenorm(x2, g, scale, shift, seq):
    t, d = x2.shape
    tm = min(256, seq)
    per = seq // tm
    row = pl.BlockSpec((tm, d), lambda i: (i, 0))
    mod = pl.BlockSpec((1, 1, d), lambda i: (i // per, 0, 0))
    return pl.pallas_call(
        _prenorm_kernel,
        grid=(t // tm,),
        in_specs=[row, pl.BlockSpec((1, d), lambda i: (0, 0)), mod, mod],
        out_specs=row,
        out_shape=jax.ShapeDtypeStruct((t, d), BF16),
        compiler_params=_params(("arbitrary",)),
        name="prenorm_modulate",
    )(x2, g.reshape(1, d), scale, shift)


def _post_kernel(x_ref, y_ref, gpost_ref, gate_ref, gpre_ref, sc_ref, sh_ref, x1_ref, h_ref):
    x1 = x_ref[...] + gate_ref[0] * _rms(y_ref[...], gpost_ref[...])
    x1_ref[...] = x1
    h = _rms(x1, gpre_ref[...]) * (1.0 + sc_ref[0]) + sh_ref[0]
    h_ref[...] = h.T.astype(h_ref.dtype)


def _post_attn(x2, y, g_post, gate, g_pre, scale, shift, seq):
    t, d = x2.shape
    tm = min(256, seq)
    per = seq // tm
    row = pl.BlockSpec((tm, d), lambda i: (i, 0))
    vec = pl.BlockSpec((1, d), lambda i: (0, 0))
    mod = pl.BlockSpec((1, 1, d), lambda i: (i // per, 0, 0))
    return pl.pallas_call(
        _post_kernel,
        grid=(t // tm,),
        in_specs=[row, row, vec, mod, vec, mod, mod],
        out_specs=[row, pl.BlockSpec((d, tm), lambda i: (0, i))],
        out_shape=[jax.ShapeDtypeStruct((t, d), F32), jax.ShapeDtypeStruct((d, t), BF16)],
        compiler_params=_params(("arbitrary",)),
        name="post_attn_norm",
    )(x2, y, g_post.reshape(1, d), gate, g_pre.reshape(1, d), scale, shift)


def _final_kernel(x_ref, yt_ref, g_ref, gate_ref, o_ref):
    o_ref[...] = x_ref[...] + gate_ref[0] * _rms(yt_ref[...].T, g_ref[...])


def _final(x1, yt, g, gate, seq):
    t, d = x1.shape
    tm = min(256, seq)
    per = seq // tm
    row = pl.BlockSpec((tm, d), lambda i: (i, 0))
    return pl.pallas_call(
        _final_kernel,
        grid=(t // tm,),
        in_specs=[row, pl.BlockSpec((d, tm), lambda i: (0, i)), pl.BlockSpec((1, d), lambda i: (0, 0)),
                  pl.BlockSpec((1, 1, d), lambda i: (i // per, 0, 0))],
        out_specs=row,
        out_shape=jax.ShapeDtypeStruct((t, d), F32),
        compiler_params=_params(("arbitrary",)),
        name="final_residual",
    )(x1, yt, g.reshape(1, d), gate)


def _mm_kernel(a_ref, w_ref, o_ref):
    o_ref[...] = jnp.dot(a_ref[...], w_ref[...], preferred_element_type=F32).astype(o_ref.dtype)


def _matmul(a, w, out_dtype, name):
    m, k = a.shape
    n = w.shape[1]
    tm = min(1024, m)
    tn = next(c for c in (1024, 896, 768, 640, 512, 384, 256, 128, n) if n % c == 0)
    return pl.pallas_call(
        _mm_kernel,
        grid=(m // tm, n // tn),
        in_specs=[pl.BlockSpec((tm, k), lambda i, j: (i, 0)),
                  pl.BlockSpec((k, tn), lambda i, j: (0, j))],
        out_specs=pl.BlockSpec((tm, tn), lambda i, j: (i, j)),
        out_shape=jax.ShapeDtypeStruct((m, n), out_dtype),
        compiler_params=_params(("arbitrary", "arbitrary")),
        name=name,
    )(a, w)


def _rope_upper(hi, cos_ref, sin_hi_ref, sin_lo_ref):
    return (hi * cos_ref[...]
            + pltpu.roll(hi, 32, axis=1) * sin_hi_ref[...]
            + pltpu.roll(hi, 96, axis=1) * sin_lo_ref[...])


def _mla_q_kernel(z_ref, g_ref, w_ref, cos_ref, s1_ref, s2_ref, o_ref, *, scale, heads):
    qn = _rms(z_ref[...], g_ref[...]).astype(BF16)
    y = jnp.dot(qn, w_ref[...], preferred_element_type=F32)
    for h in range(heads):
        lo = y[:, h * MLA_QK_PAD:h * MLA_QK_PAD + LANES]
        hi = y[:, h * MLA_QK_PAD + LANES:(h + 1) * MLA_QK_PAD]
        o_ref[h * MLA_QK_PAD:h * MLA_QK_PAD + LANES, :] = (lo * scale).T.astype(o_ref.dtype)
        o_ref[h * MLA_QK_PAD + LANES:(h + 1) * MLA_QK_PAD, :] = (
            _rope_upper(hi, cos_ref, s1_ref, s2_ref) * scale).T.astype(o_ref.dtype)


def _mla_q(z_a, g_q, w_q, rope, seq, q_rank, n_heads):
    t = z_a.shape[0]
    tm = min(512, seq)
    hb = min(4, n_heads)
    per = seq // tm
    scale = (MLA_NOPE_DIM + MLA_ROPE_DIM) ** -0.5 * math.log2(math.e)
    tab = pl.BlockSpec((tm, LANES), lambda i, j: (i % per, 0))
    return pl.pallas_call(
        functools.partial(_mla_q_kernel, scale=scale, heads=hb),
        grid=(t // tm, n_heads // hb),
        in_specs=[pl.BlockSpec((tm, q_rank), lambda i, j: (i, 0)),
                  pl.BlockSpec((1, q_rank), lambda i, j: (0, 0)),
                  pl.BlockSpec((q_rank, hb * MLA_QK_PAD), lambda i, j: (0, j)),
                  tab, tab, tab],
        out_specs=pl.BlockSpec((hb * MLA_QK_PAD, tm), lambda i, j: (j, i)),
        out_shape=jax.ShapeDtypeStruct((n_heads * MLA_QK_PAD, t), BF16),
        compiler_params=_params(("arbitrary", "arbitrary")),
        name="mla_q_proj",
    )(z_a, g_q.reshape(1, q_rank), w_q, *rope)


def _mla_kv_kernel(c_ref, pe_ref, g_ref, wk_ref, wv_ref, cos_ref, s1_ref, s2_ref,
                   k_ref, v_ref, *, heads):
    c = _rms(c_ref[...], g_ref[...]).astype(BF16)
    kn = jnp.dot(c, wk_ref[...], preferred_element_type=F32)
    v_ref[...] = jnp.dot(c, wv_ref[...], preferred_element_type=F32).T.astype(v_ref.dtype)
    pe = _rope_upper(pe_ref[...], cos_ref, s1_ref, s2_ref).astype(k_ref.dtype)
    for h in range(heads):
        k_ref[:, h * MLA_QK_PAD:h * MLA_QK_PAD + LANES] = (
            kn[:, h * LANES:(h + 1) * LANES].astype(k_ref.dtype))
        k_ref[:, h * MLA_QK_PAD + LANES:(h + 1) * MLA_QK_PAD] = pe


def _mla_kv(z_a, g_kv, w_kn, w_v, rope, seq, q_rank, kv_rank, n_heads):
    t = z_a.shape[0]
    tm = min(256, seq)
    per = seq // tm
    tk = _attn_block(seq)
    sub = tk // tm
    tab = pl.BlockSpec((tm, LANES), lambda i: (i % per, 0))
    full = lambda shape: pl.BlockSpec(shape, lambda i: (0, 0))
    return pl.pallas_call(
        functools.partial(_mla_kv_kernel, heads=n_heads),
        grid=(t // tm,),
        in_specs=[pl.BlockSpec((tm, kv_rank), lambda i: (i, q_rank // kv_rank)),
                  pl.BlockSpec((tm, LANES), lambda i: (i, (q_rank + kv_rank) // LANES)),
                  full((1, kv_rank)),
                  full((kv_rank, n_heads * MLA_NOPE_DIM)),
                  full((kv_rank, n_heads * MLA_V_DIM)),
                  tab, tab, tab],
        out_specs=[pl.BlockSpec((tm, n_heads * MLA_QK_PAD), lambda i: (i, 0)),
                   pl.BlockSpec((None, n_heads * MLA_V_DIM, tm), lambda i: (i // sub, 0, i % sub))],
        out_shape=[jax.ShapeDtypeStruct((t, n_heads * MLA_QK_PAD), BF16),
                   jax.ShapeDtypeStruct((t // tk, n_heads * MLA_V_DIM, tk), BF16)],
        compiler_params=_params(("arbitrary",)),
        name="mla_kv_proj",
    )(z_a, z_a, g_kv.reshape(1, kv_rank), w_kn, w_v, *rope)


def _attn_block(seq):
    return min(1024, seq)


def _mla_attn_kernel(qt_ref, k_ref, vt_ref, za_ref, o_ref, s_ref, m_ref, l_ref, acc_ref, *, tq):
    qi = pl.program_id(2)

    def put_scores(slot, j):
        kb = k_ref[pl.ds(pl.multiple_of(j * tq, tq), tq), :]
        s_ref[slot] = jnp.dot(kb, qt_ref[...], preferred_element_type=F32)

    def update(s, j):
        m = m_ref[...]
        m_new = jnp.maximum(m, jnp.max(s, axis=0, keepdims=True))
        alpha = jnp.exp2(m - m_new)
        p = jnp.exp2(s - m_new)
        acc_ref[...] = alpha * acc_ref[...] + jnp.dot(vt_ref[j], p.astype(BF16),
                                                      preferred_element_type=F32)
        l_ref[...] = alpha * l_ref[...] + jnp.sum(p, axis=0, keepdims=True)
        m_ref[...] = m_new

    def diagonal(slot):
        key = lax.broadcasted_iota(jnp.int32, (tq, tq), 0)
        qry = lax.broadcasted_iota(jnp.int32, (tq, tq), 1)
        update(jnp.where(key <= qry, s_ref[slot], NEG_INF), qi)

    m_ref[...] = jnp.full(m_ref.shape, NEG_INF, F32)
    l_ref[...] = jnp.zeros(l_ref.shape, F32)
    acc_ref[...] = jnp.zeros(acc_ref.shape, F32)
    put_scores(0, 0)

    def pair(jj, _):
        j = 2 * jj
        put_scores(1, j + 1)
        update(s_ref[0], j)
        put_scores(0, j + 2)
        update(s_ref[1], j + 1)
        return 0

    lax.fori_loop(0, qi // 2, pair, 0)

    @pl.when(qi % 2 == 1)
    def _():
        put_scores(1, qi)
        update(s_ref[0], qi - 1)
        diagonal(1)

    @pl.when(qi % 2 == 0)
    def _():
        diagonal(0)

    o = (acc_ref[...] / l_ref[...]).T
    o_ref[...] = (o * jax.nn.sigmoid(za_ref[...])).astype(o_ref.dtype)


def _mla_attn(qt, k, vt, z_g, batch, seq, n_heads):
    t = k.shape[0]
    tq = _attn_block(seq)
    nq = seq // tq
    return pl.pallas_call(
        functools.partial(_mla_attn_kernel, tq=tq),
        grid=(batch, n_heads, nq),
        in_specs=[pl.BlockSpec((MLA_QK_PAD, tq), lambda b, h, i: (h, b * nq + i)),
                  pl.BlockSpec((seq, MLA_QK_PAD), lambda b, h, i: (b, h)),
                  pl.BlockSpec((nq, MLA_V_DIM, tq), lambda b, h, i: (b, h, 0)),
                  pl.BlockSpec((tq, MLA_V_DIM), lambda b, h, i: (b * nq + i, h))],
        out_specs=pl.BlockSpec((tq, MLA_V_DIM), lambda b, h, i: (b * nq + i, h)),
        out_shape=jax.ShapeDtypeStruct((t, n_heads * MLA_V_DIM), BF16),
        scratch_shapes=[pltpu.VMEM((2, tq, tq), F32), pltpu.VMEM((1, tq), F32),
                        pltpu.VMEM((1, tq), F32), pltpu.VMEM((MLA_V_DIM, tq), F32)],
        compiler_params=_params(("arbitrary", "arbitrary", "arbitrary")),
        name="mla_flash_attention",
    )(qt, k, vt, z_g)


def _swa_kernel(sink_ref, q_ref, kp_ref, kc_ref, vp_ref, vc_ref, zb_ref, a_ref, o_ref,
                *, per, n_heads):
    i = pl.program_id(0)
    has_prev = (i % per) > 0
    group = n_heads // SWA_KV_HEADS
    hd = SWA_HEAD_DIM
    log2e = math.log2(math.e)
    scale = hd ** -0.5 * log2e
    span = 2 * WINDOW
    key = lax.broadcasted_iota(jnp.int32, (span, WINDOW), 0)
    qry = lax.broadcasted_iota(jnp.int32, (span, WINDOW), 1)
    dist = qry + WINDOW - key
    valid = (dist >= 0) & (dist < WINDOW) & ((key >= WINDOW) | has_prev)
    distf = dist.astype(F32)
    nt = (((1,), (1,)), ((), ()))
    tn = (((0,), (0,)), ((), ()))
    for n in range(SWA_KV_HEADS):
        ksl = slice(n * hd, (n + 1) * hd)
        kb = jnp.concatenate([kp_ref[:, ksl], kc_ref[:, ksl]], axis=0).astype(BF16)
        vb = jnp.concatenate([vp_ref[:, ksl], vc_ref[:, ksl]], axis=0).astype(BF16)
        heads = range(n * group, (n + 1) * group)
        qs = [(q_ref[:, hq * hd:(hq + 1) * hd] * scale).astype(BF16) for hq in heads]
        ss = [lax.dot_general(kb, qh, nt, preferred_element_type=F32) for qh in qs]
        ps, dens = [], []
        for hq, s in zip(heads, ss):
            slope = 2.0 ** (-8.0 * (hq + 1) / n_heads) * log2e
            sink = sink_ref[hq] * log2e
            s = jnp.where(valid, s - slope * distf, NEG_INF)
            m = jnp.maximum(jnp.max(s, axis=0, keepdims=True), sink)
            p = jnp.exp2(s - m)
            dens.append(jnp.sum(p, axis=0, keepdims=True) + jnp.exp2(sink - m))
            ps.append(p.astype(BF16))
        outs = [lax.dot_general(vb, p, tn, preferred_element_type=F32) / den
                for p, den in zip(ps, dens)]
        for g in range(0, group, 2):
            hq = n * group + g
            sl = slice(hq * hd, (hq + 2) * hd)
            o2 = jnp.concatenate([outs[g], outs[g + 1]], axis=0).T
            mix = a_ref[:, sl].astype(F32) + jax.nn.sigmoid(zb_ref[:, sl]) * o2
            o_ref[:, sl] = mix.astype(o_ref.dtype)


def _swa(z_swa, z_g, gated_a, sinks, seq, d):
    t = z_swa.shape[0]
    n_heads = d // SWA_HEAD_DIM
    kvw = SWA_KV_HEADS * SWA_HEAD_DIM
    per = seq // WINDOW
    kcol = d // kvw
    prev = lambda c: (lambda i: (jnp.maximum(i - 1, 0), c))
    cur = lambda c: (lambda i: (i, c))
    return pl.pallas_call(
        functools.partial(_swa_kernel, per=per, n_heads=n_heads),
        grid=(t // WINDOW,),
        in_specs=[pl.BlockSpec(memory_space=pltpu.SMEM),
                  pl.BlockSpec((WINDOW, d), cur(0)),
                  pl.BlockSpec((WINDOW, kvw), prev(kcol)),
                  pl.BlockSpec((WINDOW, kvw), cur(kcol)),
                  pl.BlockSpec((WINDOW, kvw), prev(kcol + 1)),
                  pl.BlockSpec((WINDOW, kvw), cur(kcol + 1)),
                  pl.BlockSpec((WINDOW, d), cur(1)),
                  pl.BlockSpec((WINDOW, d), cur(0))],
        out_specs=pl.BlockSpec((WINDOW, d), cur(0)),
        out_shape=jax.ShapeDtypeStruct((t, d), BF16),
        compiler_params=_params(("arbitrary",)),
        name="swa_sink_attention",
    )(sinks, z_swa, z_swa, z_swa, z_swa, z_swa, z_g, gated_a)


def _top16(ss, exact):
    ss = list(ss)
    rows = lax.broadcasted_iota(jnp.int32, ss[0].shape, 0).astype(F32)
    ranks = [jnp.full(s.shape, float(PEER_TOPK), F32) for s in ss]
    vals = [[] for _ in ss]
    for r in range(PEER_TOPK):
        for i, s in enumerate(ss):
            m = jnp.max(s, axis=0, keepdims=True)
            if exact:
                first = jnp.min(jnp.where(s == m, rows, float(PEER_N_KEYS)), axis=0, keepdims=True)
                hit = rows == first
            else:
                hit = s == m
            ranks[i] = jnp.where(hit, float(r), ranks[i])
            ss[i] = jnp.where(hit, NEG_INF, s)
            vals[i].append(m)
    return list(zip(vals, ranks))


def _peer_select(s1, s2, exact):
    k = PEER_TOPK
    (v1, rank1), (v2, rank2) = _top16([s1, s2], exact)
    n = s1.shape[1]
    hk = k // 2
    r8 = lax.broadcasted_iota(jnp.int32, (hk, n), 0).astype(F32)
    colsum = lambda x: jnp.sum(x, axis=0, keepdims=True)

    def stack(vals):
        out = jnp.zeros((hk, n), F32)
        for i, v in enumerate(vals):
            out = jnp.where(r8 == float(i), v, out)
        return out

    v2_lo, v2_hi, v1_hi = stack(v2[:hk]), stack(v2[hk:]), stack(v1[hk:])
    cand = [v1[0] + v2_lo, v1[0] + v2_hi]
    code = [r8, r8 + float(hk)]
    for a in range(1, hk):
        cand.append(jnp.where(r8 < float(k // (a + 1)), v1[a] + v2_lo, NEG_INF))
        code.append(r8 + float(a * k))
    cand.append(v1_hi + v2[0])
    code.append((r8 + float(hk)) * float(k))
    sel = [jnp.zeros((hk, n), F32) for _ in cand]
    for _ in range(k):
        m = jnp.max(functools.reduce(jnp.maximum, cand), axis=0, keepdims=True)
        if exact:
            first = functools.reduce(
                jnp.minimum, [jnp.where(cv == m, cd, float(k * k)) for cv, cd in zip(cand, code)])
            first = jnp.min(first, axis=0, keepdims=True)
        for i in range(len(cand)):
            hit = (code[i] == first) if exact else (cand[i] == m)
            sel[i] = jnp.where(hit, 1.0, sel[i])
            cand[i] = jnp.where(hit, NEG_INF, cand[i])
    e2_lo = jnp.exp(v2_lo - v2[0])
    e2_hi = jnp.exp(v2_hi - v2[0])
    cnt = [colsum(sel[0]) + colsum(sel[1])] + [colsum(sel[a + 1]) for a in range(1, hk)]
    mass = [colsum(sel[0] * e2_lo) + colsum(sel[1] * e2_hi)]
    mass += [colsum(sel[a + 1] * e2_lo) for a in range(1, hk)]
    z = jnp.zeros((1, n), F32)
    n1 = jnp.zeros(s1.shape, F32)
    for a in range(hk):
        z = z + jnp.exp(v1[a] - v1[0]) * mass[a]
        n1 = jnp.where(rank1 == float(a), cnt[a], n1)
    z = z + colsum(sel[-1] * jnp.exp(v1_hi - v1[0]))
    for a in range(hk, k):
        n1 = jnp.where(rank1 == float(a), sel[-1][a - hk:a - hk + 1, :], n1)
    e1 = jnp.exp(s1 - v1[0])
    e2 = jnp.exp(s2 - v2[0]) / z
    if exact:
        bad = jnp.zeros((1, n), F32)
    else:
        ranked = lambda rk: colsum(jnp.where(rk < float(k), 1.0, 0.0))
        picked = functools.reduce(lambda x, y: x + y, cnt) + colsum(sel[-1])
        bad = (jnp.abs(ranked(rank1) - float(k)) + jnp.abs(ranked(rank2) - float(k))
               + jnp.abs(picked - float(k)))
    return (n1, e1, rank2, e2), bad


def _peer_topk_kernel(q_ref, key_ref, n1_ref, e1_ref, b2_ref, e2_ref, *, sub):
    tm = q_ref.shape[1]
    half = q_ref.shape[0] // 2

    def scores(cs):
        s1 = jnp.dot(key_ref[0, 0], q_ref[:half, cs], preferred_element_type=F32)
        s2 = jnp.dot(key_ref[0, 1], q_ref[half:, cs], preferred_element_type=F32)
        return s1, s2

    def store(cs, res):
        n1, e1, b2, e2 = res
        n1_ref[:, cs] = n1
        e1_ref[:, cs] = e1
        b2_ref[:, cs] = b2.astype(b2_ref.dtype)
        e2_ref[:, cs] = e2.astype(e2_ref.dtype)

    for c in range(tm // sub):
        cs = slice(c * sub, (c + 1) * sub)
        res, bad = _peer_select(*scores(cs), exact=False)
        store(cs, res)

        @pl.when(jnp.max(bad) > 0.0)
        def _():
            store(cs, _peer_select(*scores(cs), exact=True)[0])


def _peer_topk(qt, keys):
    t = qt.shape[1]
    heads = keys.shape[0]
    qd = qt.shape[0] // heads
    tm = min(512, t)
    out = pl.BlockSpec((PEER_N_KEYS, tm), lambda i, h: (h, i))
    shape = lambda dt: jax.ShapeDtypeStruct((heads * PEER_N_KEYS, t), dt)
    return pl.pallas_call(
        functools.partial(_peer_topk_kernel, sub=min(256, tm)),
        grid=(t // tm, heads),
        in_specs=[pl.BlockSpec((qd, tm), lambda i, h: (h, i)),
                  pl.BlockSpec((1, 2, PEER_N_KEYS, qd // 2), lambda i, h: (h, 0, 0, 0))],
        out_specs=[out, out, out, out],
        out_shape=[shape(F32), shape(F32), shape(BF16), shape(BF16)],
        compiler_params=_params(("arbitrary", "arbitrary")),
        name="peer_topk",
    )(qt, keys)


def _peer_dense_kernel(ht_ref, u_ref, v_ref, b2_ref, e2_ref, *rest, heads):
    n1_refs, e1_refs = rest[:heads], rest[heads:2 * heads]
    o_ref, a_ref = rest[2 * heads:]
    c = pl.program_id(1)
    tn = u_ref.shape[0]
    nk = PEER_N_KEYS

    @pl.when(c == 0)
    def _():
        o_ref[...] = jnp.zeros_like(o_ref)

    hid = jnp.dot(u_ref[...], ht_ref[...], preferred_element_type=F32)
    act = (0.5 * hid * (1.0 + lax.erf(hid * (2.0 ** -0.5)))).astype(BF16)
    for k in range(tn // nk):
        gate = None
        for h in range(heads):
            n1 = n1_refs[h][k:k + 1, :].astype(BF16)
            e1 = e1_refs[h][k:k + 1, :].astype(BF16)
            b2 = b2_ref[h * nk:(h + 1) * nk, :]
            e2 = e2_ref[h * nk:(h + 1) * nk, :]
            term = jnp.where(b2 < n1, e2, jnp.zeros_like(e2)) * e1
            gate = term if gate is None else gate + term
        a_ref[k * nk:(k + 1) * nk, :] = act[k * nk:(k + 1) * nk, :] * gate
    o_ref[...] += lax.dot_general(v_ref[...], a_ref[...], (((0,), (0,)), ((), ())),
                                  preferred_element_type=F32)


def _peer_dense(ht, u, v, n1, e1, b2, e2, heads):
    d, t = ht.shape
    n_exp = u.shape[0]
    tm = min(512, t)
    tn = min(1024, n_exp)
    rows = heads * PEER_N_KEYS
    per_chunk = tn // PEER_N_KEYS
    chunks_per_head = PEER_N_KEYS // per_chunk
    once = pl.Buffered(1)
    sel = pl.BlockSpec((rows, tm), lambda i, c: (0, i), pipeline_mode=once)
    head_rows = [pl.BlockSpec((per_chunk, tm), lambda i, c, h=h: (h * chunks_per_head + c, i))
                 for h in range(heads)]
    return pl.pallas_call(
        functools.partial(_peer_dense_kernel, heads=heads),
        grid=(t // tm, n_exp // tn),
        in_specs=[pl.BlockSpec((d, tm), lambda i, c: (0, i), pipeline_mode=once),
                  pl.BlockSpec((tn, d), lambda i, c: (c, 0)),
                  pl.BlockSpec((tn, d), lambda i, c: (c, 0)),
                  sel, sel] + head_rows + head_rows,
        out_specs=pl.BlockSpec((d, tm), lambda i, c: (0, i), pipeline_mode=once),
        out_shape=jax.ShapeDtypeStruct((d, t), F32),
        scratch_shapes=[pltpu.VMEM((tn, tm), BF16)],
        compiler_params=_params(("arbitrary", "arbitrary"), vmem=VMEM_LIMIT_LARGE),
        name="peer_dense_experts",
    )(ht, u, v, b2, e2, *([n1] * heads), *([e1] * heads))


def _rope_tables(seq):
    half = MLA_ROPE_DIM // 2
    pos = jnp.arange(seq, dtype=F32)
    inv_freq = ROPE_THETA ** (-jnp.arange(0, MLA_ROPE_DIM, 2, dtype=F32) / MLA_ROPE_DIM)
    ang = pos[:, None] * inv_freq[None, :]
    cos, sin = jnp.cos(ang), jnp.sin(ang)
    zero = jnp.zeros((seq, half), F32)
    pad = jnp.zeros((seq, LANES - MLA_ROPE_DIM), F32)
    cos_t = jnp.concatenate([cos, cos, pad], axis=1)
    sin_hi = jnp.concatenate([zero, sin, pad], axis=1)
    sin_lo = jnp.concatenate([-sin, zero, pad], axis=1)
    return cos_t, sin_hi, sin_lo


def _layer(x2, cond_in, batch, seq, w_ada, b_ada, g_attn_pre, g_attn_post, w_in, g_q_lat, w_q_b,
           g_kv_lat, w_kv_b, sinks, w_out, g_ffn_pre, g_ffn_post, w_peer_q, sub_keys, peer_u, peer_v):
    t, d = x2.shape
    q_rank = g_q_lat.shape[0]
    kv_rank = g_kv_lat.shape[0]
    n_mla = d // MLA_V_DIM
    kvw = SWA_KV_HEADS * SWA_HEAD_DIM
    heads = sub_keys.shape[0]

    ada = _ada(cond_in, w_ada, b_ada).reshape(batch, N_ADA, 1, d)
    shift_a, scale_a, gate_a, shift_f, scale_f, gate_f = [ada[:, i] for i in range(N_ADA)]

    w_a_cols = q_rank + kv_rank + MLA_ROPE_DIM
    w_a = jnp.pad(w_in[:, :w_a_cols], ((0, 0), (0, -w_a_cols % (2 * LANES)))).astype(BF16)
    off_swa = w_a_cols
    off_gate = off_swa + d + 2 * kvw
    w_swa = w_in[:, off_swa:off_gate].astype(BF16)
    w_g = w_in[:, off_gate:].astype(BF16)
    qk = MLA_NOPE_DIM + MLA_ROPE_DIM
    w_q = jnp.pad(w_q_b.reshape(q_rank, n_mla, qk),
                  ((0, 0), (0, 0), (0, MLA_QK_PAD - qk))).reshape(q_rank, n_mla * MLA_QK_PAD).astype(BF16)
    w_kv = w_kv_b.reshape(kv_rank, n_mla, MLA_NOPE_DIM + MLA_V_DIM)
    w_kn = w_kv[:, :, :MLA_NOPE_DIM].reshape(kv_rank, n_mla * MLA_NOPE_DIM).astype(BF16)
    w_v = w_kv[:, :, MLA_NOPE_DIM:].reshape(kv_rank, n_mla * MLA_V_DIM).astype(BF16)
    rope = _rope_tables(seq)

    h = _prenorm(x2, g_attn_pre, scale_a, shift_a, seq)
    z_a = _matmul(h, w_a, F32, name="in_proj_mla")
    z_swa = _matmul(h, w_swa, F32, name="in_proj_swa")
    z_g = _matmul(h, w_g, F32, name="in_proj_gates")
    q = _mla_q(z_a, g_q_lat, w_q, rope, seq, q_rank, n_mla)
    k, v = _mla_kv(z_a, g_kv_lat, w_kn, w_v, rope, seq, q_rank, kv_rank, n_mla)
    gated_a = _mla_attn(q, k, v, z_g, batch, seq, n_mla)
    mix = _swa(z_swa, z_g, gated_a, sinks, seq, d)
    y = _matmul(mix, w_out.astype(BF16), F32, name="out_proj")
    x1, h2t = _post_attn(x2, y, g_attn_post, gate_a, g_ffn_pre, scale_f, shift_f, seq)

    pqt = _matmul(w_peer_q.T.astype(BF16), h2t, BF16, name="peer_query")
    n1, e1, b2, e2 = _peer_topk(pqt, sub_keys.astype(BF16))
    yt = _peer_dense(h2t, peer_u.astype(BF16), peer_v.astype(BF16), n1, e1, b2, e2, heads)
    return _final(x1, yt, g_ffn_post, gate_f, seq)


def kernel(x, c, w_ada, b_ada, g_attn_pre, g_attn_post, w_in, g_q_lat, w_q_b, g_kv_lat, w_kv_b, sinks, w_out, g_ffn_pre, g_ffn_post, w_peer_q, peer_sub_keys, peer_u, peer_v):
    batch, seq, d = x.shape
    x2 = x.reshape(batch * seq, d)
    for l in range(w_ada.shape[0]):
        x2 = _layer(x2, c, batch, seq, w_ada[l], b_ada[l], g_attn_pre[l], g_attn_post[l], w_in[l],
                    g_q_lat[l], w_q_b[l], g_kv_lat[l], w_kv_b[l], sinks[l], w_out[l], g_ffn_pre[l],
                    g_ffn_post[l], w_peer_q[l], peer_sub_keys[l], peer_u[l], peer_v[l])
    return x2.reshape(batch, seq, d)
```

```python
import functools
import math

import jax
import jax.numpy as jnp
from jax import lax
from jax.experimental import pallas as pl
from jax.experimental.pallas import tpu as pltpu

F32 = jnp.float32
BF16 = jnp.bfloat16

MLA_NOPE_DIM = 128
MLA_ROPE_DIM = 64
MLA_V_DIM = 128
MLA_QK_PAD = 256
ROPE_THETA = 10000.0
SWA_HEAD_DIM = 64
SWA_KV_HEADS = 8
WINDOW = 128
PEER_HEADS = 8
PEER_N_KEYS = 128
PEER_TOPK = 16
N_ADA = 6
NORM_EPS = 1e-6
NEG_INF = -1e30

LANES = 128
VMEM_LIMIT = 56 * 1024 * 1024
VMEM_LIMIT_LARGE = 60 * 1024 * 1024


def _params(sem, vmem=VMEM_LIMIT):
    return pltpu.CompilerParams(dimension_semantics=sem, vmem_limit_bytes=vmem)


def _rms(x, g):
    return x * lax.rsqrt(jnp.mean(x * x, axis=-1, keepdims=True) + NORM_EPS) * g


def _ada_kernel(cb_ref, w_ref, b_ref, o_ref):
    nb = cb_ref.shape[0]
    tn = w_ref.shape[1]
    for b in range(nb):
        cb = cb_ref[b]
        cond = cb * jax.nn.sigmoid(cb)
        for j in range(tn // LANES):
            sl = slice(j * LANES, (j + 1) * LANES)
            r = jnp.sum(w_ref[:, sl] * cond, axis=0, keepdims=True)
            o_ref[b:b + 1, sl] = r + b_ref[:, sl]


def _ada(c, w, bias):
    nb, k = c.shape
    n = w.shape[1]
    tn = min(512, n)
    cb = jnp.broadcast_to(c[:, :, None], (nb, k, LANES))
    return pl.pallas_call(
        _ada_kernel,
        grid=(n // tn,),
        in_specs=[pl.BlockSpec((nb, k, LANES), lambda j: (0, 0, 0)),
                  pl.BlockSpec((k, tn), lambda j: (0, j)),
                  pl.BlockSpec((1, tn), lambda j: (0, j))],
        out_specs=pl.BlockSpec((nb, tn), lambda j: (0, j)),
        out_shape=jax.ShapeDtypeStruct((nb, n), F32),
        compiler_params=_params(("arbitrary",)),
        name="ada_matvec",
    )(cb, w, bias.reshape(1, n))


def _prenorm_kernel(x_ref, g_ref, sc_ref, sh_ref, o_ref):
    h = _rms(x_ref[...], g_ref[...]) * (1.0 + sc_ref[0]) + sh_ref[0]
    o_ref[...] = h.astype(o_ref.dtype)


def _prenorm(x2, g, scale, shift, seq):
    t, d = x2.shape
    tm = min(256, seq)
    per = seq // tm
    row = pl.BlockSpec((tm, d), lambda i: (i, 0))
    mod = pl.BlockSpec((1, 1, d), lambda i: (i // per, 0, 0))
    return pl.pallas_call(
        _prenorm_kernel,
        grid=(t // tm,),
        in_specs=[row, pl.BlockSpec((1, d), lambda i: (0, 0)), mod, mod],
        out_specs=row,
        out_shape=jax.ShapeDtypeStruct((t, d), BF16),
        compiler_params=_params(("arbitrary",)),
        name="prenorm_modulate",
    )(x2, g.reshape(1, d), scale, shift)


def _post_kernel(x_ref, y_ref, gpost_ref, gate_ref, gpre_ref, sc_ref, sh_ref, x1_ref, h_ref):
    x1 = x_ref[...] + gate_ref[0] * _rms(y_ref[...], gpost_ref[...])
    x1_ref[...] = x1
    h = _rms(x1, gpre_ref[...]) * (1.0 + sc_ref[0]) + sh_ref[0]
    h_ref[...] = h.T.astype(h_ref.dtype)


def _post_attn(x2, y, g_post, gate, g_pre, scale, shift, seq):
    t, d = x2.shape
    tm = min(256, seq)
    per = seq // tm
    row = pl.BlockSpec((tm, d), lambda i: (i, 0))
    vec = pl.BlockSpec((1, d), lambda i: (0, 0))
    mod = pl.BlockSpec((1, 1, d), lambda i: (i // per, 0, 0))
    return pl.pallas_call(
        _post_kernel,
        grid=(t // tm,),
        in_specs=[row, row, vec, mod, vec, mod, mod],
        out_specs=[row, pl.BlockSpec((d, tm), lambda i: (0, i))],
        out_shape=[jax.ShapeDtypeStruct((t, d), F32), jax.ShapeDtypeStruct((d, t), BF16)],
        compiler_params=_params(("arbitrary",)),
        name="post_attn_norm",
    )(x2, y, g_post.reshape(1, d), gate, g_pre.reshape(1, d), scale, shift)


def _final_kernel(x_ref, yt_ref, g_ref, gate_ref, o_ref):
    o_ref[...] = x_ref[...] + gate_ref[0] * _rms(yt_ref[...].T, g_ref[...])


def _final(x1, yt, g, gate, seq):
    t, d = x1.shape
    tm = min(256, seq)
    per = seq // tm
    row = pl.BlockSpec((tm, d), lambda i: (i, 0))
    return pl.pallas_call(
        _final_kernel,
        grid=(t // tm,),
        in_specs=[row, pl.BlockSpec((d, tm), lambda i: (0, i)), pl.BlockSpec((1, d), lambda i: (0, 0)),
                  pl.BlockSpec((1, 1, d), lambda i: (i // per, 0, 0))],
        out_specs=row,
        out_shape=jax.ShapeDtypeStruct((t, d), F32),
        compiler_params=_params(("arbitrary",)),
        name="final_residual",
    )(x1, yt, g.reshape(1, d), gate)


def _mm_kernel(a_ref, w_ref, o_ref):
    o_ref[...] = jnp.dot(a_ref[...], w_ref[...], preferred_element_type=F32).astype(o_ref.dtype)


def _matmul(a, w, out_dtype, name):
    m, k = a.shape
    n = w.shape[1]
    tm = min(1024, m)
    tn = next(c for c in (1024, 896, 768, 640, 512, 384, 256, 128, n) if n % c == 0)
    return pl.pallas_call(
        _mm_kernel,
        grid=(m // tm, n // tn),
        in_specs=[pl.BlockSpec((tm, k), lambda i, j: (i, 0)),
                  pl.BlockSpec((k, tn), lambda i, j: (0, j))],
        out_specs=pl.BlockSpec((tm, tn), lambda i, j: (i, j)),
        out_shape=jax.ShapeDtypeStruct((m, n), out_dtype),
        compiler_params=_params(("arbitrary", "arbitrary")),
        name=name,
    )(a, w)


def _rope_upper(hi, cos_ref, sin_hi_ref, sin_lo_ref):
    return (hi * cos_ref[...]
            + pltpu.roll(hi, 32, axis=1) * sin_hi_ref[...]
            + pltpu.roll(hi, 96, axis=1) * sin_lo_ref[...])


def _mla_q_kernel(z_ref, g_ref, w_ref, cos_ref, s1_ref, s2_ref, o_ref, *, scale, heads):
    qn = _rms(z_ref[...], g_ref[...]).astype(BF16)
    y = jnp.dot(qn, w_ref[...], preferred_element_type=F32)
    for h in range(heads):
        lo = y[:, h * MLA_QK_PAD:h * MLA_QK_PAD + LANES]
        hi = y[:, h * MLA_QK_PAD + LANES:(h + 1) * MLA_QK_PAD]
        o_ref[h * MLA_QK_PAD:h * MLA_QK_PAD + LANES, :] = (lo * scale).T.astype(o_ref.dtype)
        o_ref[h * MLA_QK_PAD + LANES:(h + 1) * MLA_QK_PAD, :] = (
            _rope_upper(hi, cos_ref, s1_ref, s2_ref) * scale).T.astype(o_ref.dtype)


def _mla_q(z_a, g_q, w_q, rope, seq, q_rank, n_heads):
    t = z_a.shape[0]
    tm = min(512, seq)
    hb = min(4, n_heads)
    per = seq // tm
    scale = (MLA_NOPE_DIM + MLA_ROPE_DIM) ** -0.5 * math.log2(math.e)
    tab = pl.BlockSpec((tm, LANES), lambda i, j: (i % per, 0))
    return pl.pallas_call(
        functools.partial(_mla_q_kernel, scale=scale, heads=hb),
        grid=(t // tm, n_heads // hb),
        in_specs=[pl.BlockSpec((tm, q_rank), lambda i, j: (i, 0)),
                  pl.BlockSpec((1, q_rank), lambda i, j: (0, 0)),
                  pl.BlockSpec((q_rank, hb * MLA_QK_PAD), lambda i, j: (0, j)),
                  tab, tab, tab],
        out_specs=pl.BlockSpec((hb * MLA_QK_PAD, tm), lambda i, j: (j, i)),
        out_shape=jax.ShapeDtypeStruct((n_heads * MLA_QK_PAD, t), BF16),
        compiler_params=_params(("arbitrary", "arbitrary")),
        name="mla_q_proj",
    )(z_a, g_q.reshape(1, q_rank), w_q, *rope)


def _mla_kv_kernel(c_ref, pe_ref, g_ref, wk_ref, wv_ref, cos_ref, s1_ref, s2_ref,
                   k_ref, v_ref, *, heads):
    c = _rms(c_ref[...], g_ref[...]).astype(BF16)
    kn = jnp.dot(c, wk_ref[...], preferred_element_type=F32)
    v_ref[...] = jnp.dot(c, wv_ref[...], preferred_element_type=F32).T.astype(v_ref.dtype)
    pe = _rope_upper(pe_ref[...], cos_ref, s1_ref, s2_ref).astype(k_ref.dtype)
    for h in range(heads):
        k_ref[:, h * MLA_QK_PAD:h * MLA_QK_PAD + LANES] = (
            kn[:, h * LANES:(h + 1) * LANES].astype(k_ref.dtype))
        k_ref[:, h * MLA_QK_PAD + LANES:(h + 1) * MLA_QK_PAD] = pe


def _mla_kv(z_a, g_kv, w_kn, w_v, rope, seq, q_rank, kv_rank, n_heads):
    t = z_a.shape[0]
    tm = min(256, seq)
    per = seq // tm
    tk = _attn_block(seq)
    sub = tk // tm
    tab = pl.BlockSpec((tm, LANES), lambda i: (i % per, 0))
    full = lambda shape: pl.BlockSpec(shape, lambda i: (0, 0))
    return pl.pallas_call(
        functools.partial(_mla_kv_kernel, heads=n_heads),
        grid=(t // tm,),
        in_specs=[pl.BlockSpec((tm, kv_rank), lambda i: (i, q_rank // kv_rank)),
                  pl.BlockSpec((tm, LANES), lambda i: (i, (q_rank + kv_rank) // LANES)),
                  full((1, kv_rank)),
                  full((kv_rank, n_heads * MLA_NOPE_DIM)),
                  full((kv_rank, n_heads * MLA_V_DIM)),
                  tab, tab, tab],
        out_specs=[pl.BlockSpec((tm, n_heads * MLA_QK_PAD), lambda i: (i, 0)),
                   pl.BlockSpec((None, n_heads * MLA_V_DIM, tm), lambda i: (i // sub, 0, i % sub))],
        out_shape=[jax.ShapeDtypeStruct((t, n_heads * MLA_QK_PAD), BF16),
                   jax.ShapeDtypeStruct((t // tk, n_heads * MLA_V_DIM, tk), BF16)],
        compiler_params=_params(("arbitrary",)),
        name="mla_kv_proj",
    )(z_a, z_a, g_kv.reshape(1, kv_rank), w_kn, w_v, *rope)


def _attn_block(seq):
    return min(1024, seq)


def _mla_attn_kernel(qt_ref, k_ref, vt_ref, za_ref, o_ref, s_ref, m_ref, l_ref, acc_ref, *, tq, nq):
    def put_scores(slot, qi, j):
        kb = k_ref[pl.ds(pl.multiple_of(j * tq, tq), tq), :]
        s_ref[slot] = jnp.dot(kb, qt_ref[:, qi * tq:(qi + 1) * tq],
                              preferred_element_type=F32)

    def update(s, j):
        m = m_ref[...]
        m_new = jnp.maximum(m, jnp.max(s, axis=0, keepdims=True))
        alpha = jnp.exp2(m - m_new)
        p = jnp.exp2(s - m_new)
        acc_ref[...] = alpha * acc_ref[...] + jnp.dot(vt_ref[j], p.astype(BF16),
                                                      preferred_element_type=F32)
        l_ref[...] = alpha * l_ref[...] + jnp.sum(p, axis=0, keepdims=True)
        m_ref[...] = m_new

    def finish(slot, qi):
        key = lax.broadcasted_iota(jnp.int32, (tq, tq), 0)
        qry = lax.broadcasted_iota(jnp.int32, (tq, tq), 1)
        update(jnp.where(key <= qry, s_ref[slot], NEG_INF), qi)
        rows = slice(qi * tq, (qi + 1) * tq)
        o = (acc_ref[...] / l_ref[...]).T
        o_ref[rows, :] = (o * jax.nn.sigmoid(za_ref[rows, :])).astype(o_ref.dtype)

    put_scores(0, 0, 0)
    first = 0
    for qi in range(nq):
        a, b = first, 1 - first
        m_ref[...] = jnp.full(m_ref.shape, NEG_INF, F32)
        l_ref[...] = jnp.zeros(l_ref.shape, F32)
        acc_ref[...] = jnp.zeros(acc_ref.shape, F32)

        def pair(jj, _, qi=qi, a=a, b=b):
            j = 2 * jj
            put_scores(b, qi, j + 1)
            update(s_ref[a], j)
            put_scores(a, qi, j + 2)
            update(s_ref[b], j + 1)
            return 0

        if qi >= 2:
            lax.fori_loop(0, qi // 2, pair, 0)
        if qi % 2 == 1:
            put_scores(b, qi, qi)
            update(s_ref[a], qi - 1)
            diag, free = b, a
        else:
            diag, free = a, b
        if qi + 1 < nq:
            put_scores(free, qi + 1, 0)
        finish(diag, qi)
        first = free


def _mla_attn(qt, k, vt, z_g, batch, seq, n_heads):
    t = k.shape[0]
    tq = _attn_block(seq)
    nq = seq // tq
    return pl.pallas_call(
        functools.partial(_mla_attn_kernel, tq=tq, nq=nq),
        grid=(batch, n_heads),
        in_specs=[pl.BlockSpec((MLA_QK_PAD, seq), lambda b, h: (h, b)),
                  pl.BlockSpec((seq, MLA_QK_PAD), lambda b, h: (b, h)),
                  pl.BlockSpec((nq, MLA_V_DIM, tq), lambda b, h: (b, h, 0)),
                  pl.BlockSpec((seq, MLA_V_DIM), lambda b, h: (b, h))],
        out_specs=pl.BlockSpec((seq, MLA_V_DIM), lambda b, h: (b, h)),
        out_shape=jax.ShapeDtypeStruct((t, n_heads * MLA_V_DIM), BF16),
        scratch_shapes=[pltpu.VMEM((2, tq, tq), F32), pltpu.VMEM((1, tq), F32),
                        pltpu.VMEM((1, tq), F32), pltpu.VMEM((MLA_V_DIM, tq), F32)],
        compiler_params=_params(("arbitrary", "arbitrary")),
        name="mla_flash_attention",
    )(qt, k, vt, z_g)


def _swa_kernel(sink_ref, q_ref, kp_ref, kc_ref, vp_ref, vc_ref, zb_ref, a_ref, o_ref,
                *, per, n_heads):
    i = pl.program_id(0)
    has_prev = (i % per) > 0
    group = n_heads // SWA_KV_HEADS
    hd = SWA_HEAD_DIM
    log2e = math.log2(math.e)
    scale = hd ** -0.5 * log2e
    span = 2 * WINDOW
    key = lax.broadcasted_iota(jnp.int32, (span, WINDOW), 0)
    qry = lax.broadcasted_iota(jnp.int32, (span, WINDOW), 1)
    dist = qry + WINDOW - key
    valid = (dist >= 0) & (dist < WINDOW) & ((key >= WINDOW) | has_prev)
    distf = dist.astype(F32)
    nt = (((1,), (1,)), ((), ()))
    tn = (((0,), (0,)), ((), ()))
    for n in range(SWA_KV_HEADS):
        ksl = slice(n * hd, (n + 1) * hd)
        kb = jnp.concatenate([kp_ref[:, ksl], kc_ref[:, ksl]], axis=0).astype(BF16)
        vb = jnp.concatenate([vp_ref[:, ksl], vc_ref[:, ksl]], axis=0).astype(BF16)
        heads = range(n * group, (n + 1) * group)
        qs = [(q_ref[:, hq * hd:(hq + 1) * hd] * scale).astype(BF16) for hq in heads]
        ss = [lax.dot_general(kb, qh, nt, preferred_element_type=F32) for qh in qs]
        ps, dens = [], []
        for hq, s in zip(heads, ss):
            slope = 2.0 ** (-8.0 * (hq + 1) / n_heads) * log2e
            sink = sink_ref[hq] * log2e
            s = jnp.where(valid, s - slope * distf, NEG_INF)
            m = jnp.maximum(jnp.max(s, axis=0, keepdims=True), sink)
            p = jnp.exp2(s - m)
            dens.append(jnp.sum(p, axis=0, keepdims=True) + jnp.exp2(sink - m))
            ps.append(p.astype(BF16))
        outs = [lax.dot_general(vb, p, tn, preferred_element_type=F32) / den
                for p, den in zip(ps, dens)]
        for g in range(0, group, 2):
            hq = n * group + g
            sl = slice(hq * hd, (hq + 2) * hd)
            o2 = jnp.concatenate([outs[g], outs[g + 1]], axis=0).T
            mix = a_ref[:, sl].astype(F32) + jax.nn.sigmoid(zb_ref[:, sl]) * o2
            o_ref[:, sl] = mix.astype(o_ref.dtype)


def _swa(z_swa, z_g, gated_a, sinks, seq, d):
    t = z_swa.shape[0]
    n_heads = d // SWA_HEAD_DIM
    kvw = SWA_KV_HEADS * SWA_HEAD_DIM
    per = seq // WINDOW
    kcol = d // kvw
    prev = lambda c: (lambda i: (jnp.maximum(i - 1, 0), c))
    cur = lambda c: (lambda i: (i, c))
    return pl.pallas_call(
        functools.partial(_swa_kernel, per=per, n_heads=n_heads),
        grid=(t // WINDOW,),
        in_specs=[pl.BlockSpec(memory_space=pltpu.SMEM),
                  pl.BlockSpec((WINDOW, d), cur(0)),
                  pl.BlockSpec((WINDOW, kvw), prev(kcol)),
                  pl.BlockSpec((WINDOW, kvw), cur(kcol)),
                  pl.BlockSpec((WINDOW, kvw), prev(kcol + 1)),
                  pl.BlockSpec((WINDOW, kvw), cur(kcol + 1)),
                  pl.BlockSpec((WINDOW, d), cur(1)),
                  pl.BlockSpec((WINDOW, d), cur(0))],
        out_specs=pl.BlockSpec((WINDOW, d), cur(0)),
        out_shape=jax.ShapeDtypeStruct((t, d), BF16),
        compiler_params=_params(("arbitrary",)),
        name="swa_sink_attention",
    )(sinks, z_swa, z_swa, z_swa, z_swa, z_swa, z_g, gated_a)


def _top16(ss, exact):
    ss = list(ss)
    rows = lax.broadcasted_iota(jnp.int32, ss[0].shape, 0).astype(F32)
    ranks = [jnp.full(s.shape, float(PEER_TOPK), F32) for s in ss]
    vals = [[] for _ in ss]
    for r in range(PEER_TOPK):
        for i, s in enumerate(ss):
            m = jnp.max(s, axis=0, keepdims=True)
            if exact:
                first = jnp.min(jnp.where(s == m, rows, float(PEER_N_KEYS)), axis=0, keepdims=True)
                hit = rows == first
            else:
                hit = s == m
            ranks[i] = jnp.where(hit, float(r), ranks[i])
            ss[i] = jnp.where(hit, NEG_INF, s)
            vals[i].append(m)
    return list(zip(vals, ranks))


def _peer_select(s1, s2, exact):
    k = PEER_TOPK
    (v1, rank1), (v2, rank2) = _top16([s1, s2], exact)
    n = s1.shape[1]
    hk = k // 2
    r8 = lax.broadcasted_iota(jnp.int32, (hk, n), 0).astype(F32)
    colsum = lambda x: jnp.sum(x, axis=0, keepdims=True)

    def stack(vals):
        out = jnp.zeros((hk, n), F32)
        for i, v in enumerate(vals):
            out = jnp.where(r8 == float(i), v, out)
        return out

    v2_lo, v2_hi, v1_hi = stack(v2[:hk]), stack(v2[hk:]), stack(v1[hk:])
    cand = [v1[0] + v2_lo, v1[0] + v2_hi]
    code = [r8, r8 + float(hk)]
    for a in range(1, hk):
        cand.append(jnp.where(r8 < float(k // (a + 1)), v1[a] + v2_lo, NEG_INF))
        code.append(r8 + float(a * k))
    cand.append(v1_hi + v2[0])
    code.append((r8 + float(hk)) * float(k))
    sel = [jnp.zeros((hk, n), F32) for _ in cand]
    for _ in range(k):
        m = jnp.max(functools.reduce(jnp.maximum, cand), axis=0, keepdims=True)
        if exact:
            first = functools.reduce(
                jnp.minimum, [jnp.where(cv == m, cd, float(k * k)) for cv, cd in zip(cand, code)])
            first = jnp.min(first, axis=0, keepdims=True)
        for i in range(len(cand)):
            hit = (code[i] == first) if exact else (cand[i] == m)
            sel[i] = jnp.where(hit, 1.0, sel[i])
            cand[i] = jnp.where(hit, NEG_INF, cand[i])
    e2_lo = jnp.exp(v2_lo - v2[0])
    e2_hi = jnp.exp(v2_hi - v2[0])
    cnt = [colsum(sel[0]) + colsum(sel[1])] + [colsum(sel[a + 1]) for a in range(1, hk)]
    mass = [colsum(sel[0] * e2_lo) + colsum(sel[1] * e2_hi)]
    mass += [colsum(sel[a + 1] * e2_lo) for a in range(1, hk)]
    z = jnp.zeros((1, n), F32)
    n1 = jnp.zeros(s1.shape, F32)
    for a in range(hk):
        z = z + jnp.exp(v1[a] - v1[0]) * mass[a]
        n1 = jnp.where(rank1 == float(a), cnt[a], n1)
    z = z + colsum(sel[-1] * jnp.exp(v1_hi - v1[0]))
    for a in range(hk, k):
        n1 = jnp.where(rank1 == float(a), sel[-1][a - hk:a - hk + 1, :], n1)
    e1 = jnp.exp(s1 - v1[0])
    e2 = jnp.exp(s2 - v2[0]) / z
    if exact:
        bad = jnp.zeros((1, n), F32)
    else:
        ranked = lambda rk: colsum(jnp.where(rk < float(k), 1.0, 0.0))
        picked = functools.reduce(lambda x, y: x + y, cnt) + colsum(sel[-1])
        bad = (jnp.abs(ranked(rank1) - float(k)) + jnp.abs(ranked(rank2) - float(k))
               + jnp.abs(picked - float(k)))
    return (n1, e1, rank2, e2), bad


def _peer_topk_kernel(q_ref, key_ref, n1_ref, e1_ref, b2_ref, e2_ref, *, sub):
    tm = q_ref.shape[1]
    half = q_ref.shape[0] // 2

    def scores(cs):
        s1 = jnp.dot(key_ref[0, 0], q_ref[:half, cs], preferred_element_type=F32)
        s2 = jnp.dot(key_ref[0, 1], q_ref[half:, cs], preferred_element_type=F32)
        return s1, s2

    def store(cs, res):
        n1, e1, b2, e2 = res
        n1_ref[:, cs] = n1
        e1_ref[:, cs] = e1
        b2_ref[:, cs] = b2.astype(b2_ref.dtype)
        e2_ref[:, cs] = e2.astype(e2_ref.dtype)

    for c in range(tm // sub):
        cs = slice(c * sub, (c + 1) * sub)
        res, bad = _peer_select(*scores(cs), exact=False)
        store(cs, res)

        @pl.when(jnp.max(bad) > 0.0)
        def _():
            store(cs, _peer_select(*scores(cs), exact=True)[0])


def _peer_topk(qt, keys):
    t = qt.shape[1]
    heads = keys.shape[0]
    qd = qt.shape[0] // heads
    tm = min(512, t)
    out = pl.BlockSpec((PEER_N_KEYS, tm), lambda i, h: (h, i))
    shape = lambda dt: jax.ShapeDtypeStruct((heads * PEER_N_KEYS, t), dt)
    return pl.pallas_call(
        functools.partial(_peer_topk_kernel, sub=min(256, tm)),
        grid=(t // tm, heads),
        in_specs=[pl.BlockSpec((qd, tm), lambda i, h: (h, i)),
                  pl.BlockSpec((1, 2, PEER_N_KEYS, qd // 2), lambda i, h: (h, 0, 0, 0))],
        out_specs=[out, out, out, out],
        out_shape=[shape(F32), shape(F32), shape(BF16), shape(BF16)],
        compiler_params=_params(("arbitrary", "arbitrary")),
        name="peer_topk",
    )(qt, keys)


def _peer_dense_kernel(ht_ref, u_ref, v_ref, b2_ref, e2_ref, *rest, heads):
    n1_refs, e1_refs = rest[:heads], rest[heads:2 * heads]
    o_ref, a_ref = rest[2 * heads:]
    c = pl.program_id(1)
    tn = u_ref.shape[0]
    nk = PEER_N_KEYS

    @pl.when(c == 0)
    def _():
        o_ref[...] = jnp.zeros_like(o_ref)

    hid = jnp.dot(u_ref[...], ht_ref[...], preferred_element_type=F32)
    act = (0.5 * hid * (1.0 + lax.erf(hid * (2.0 ** -0.5)))).astype(BF16)
    for k in range(tn // nk):
        gate = None
        for h in range(heads):
            n1 = n1_refs[h][k:k + 1, :].astype(BF16)
            e1 = e1_refs[h][k:k + 1, :].astype(BF16)
            b2 = b2_ref[h * nk:(h + 1) * nk, :]
            e2 = e2_ref[h * nk:(h + 1) * nk, :]
            term = jnp.where(b2 < n1, e2, jnp.zeros_like(e2)) * e1
            gate = term if gate is None else gate + term
        a_ref[k * nk:(k + 1) * nk, :] = act[k * nk:(k + 1) * nk, :] * gate
    o_ref[...] += lax.dot_general(v_ref[...], a_ref[...], (((0,), (0,)), ((), ())),
                                  preferred_element_type=F32)


def _peer_dense(ht, u, v, n1, e1, b2, e2, heads):
    d, t = ht.shape
    n_exp = u.shape[0]
    tm = min(512, t)
    tn = min(1024, n_exp)
    rows = heads * PEER_N_KEYS
    per_chunk = tn // PEER_N_KEYS
    chunks_per_head = PEER_N_KEYS // per_chunk
    once = pl.Buffered(1)
    sel = pl.BlockSpec((rows, tm), lambda i, c: (0, i), pipeline_mode=once)
    head_rows = [pl.BlockSpec((per_chunk, tm), lambda i, c, h=h: (h * chunks_per_head + c, i))
                 for h in range(heads)]
    return pl.pallas_call(
        functools.partial(_peer_dense_kernel, heads=heads),
        grid=(t // tm, n_exp // tn),
        in_specs=[pl.BlockSpec((d, tm), lambda i, c: (0, i), pipeline_mode=once),
                  pl.BlockSpec((tn, d), lambda i, c: (c, 0)),
                  pl.BlockSpec((tn, d), lambda i, c: (c, 0)),
                  sel, sel] + head_rows + head_rows,
        out_specs=pl.BlockSpec((d, tm), lambda i, c: (0, i), pipeline_mode=once),
        out_shape=jax.ShapeDtypeStruct((d, t), F32),
        scratch_shapes=[pltpu.VMEM((tn, tm), BF16)],
        compiler_params=_params(("arbitrary", "arbitrary"), vmem=VMEM_LIMIT_LARGE),
        name="peer_dense_experts",
    )(ht, u, v, b2, e2, *([n1] * heads), *([e1] * heads))


def _rope_tables(seq):
    half = MLA_ROPE_DIM // 2
    pos = jnp.arange(seq, dtype=F32)
    inv_freq = ROPE_THETA ** (-jnp.arange(0, MLA_ROPE_DIM, 2, dtype=F32) / MLA_ROPE_DIM)
    ang = pos[:, None] * inv_freq[None, :]
    cos, sin = jnp.cos(ang), jnp.sin(ang)
    zero = jnp.zeros((seq, half), F32)
    pad = jnp.zeros((seq, LANES - MLA_ROPE_DIM), F32)
    cos_t = jnp.concatenate([cos, cos, pad], axis=1)
    sin_hi = jnp.concatenate([zero, sin, pad], axis=1)
    sin_lo = jnp.concatenate([-sin, zero, pad], axis=1)
    return cos_t, sin_hi, sin_lo


def _layer(x2, cond_in, batch, seq, w_ada, b_ada, g_attn_pre, g_attn_post, w_in, g_q_lat, w_q_b,
           g_kv_lat, w_kv_b, sinks, w_out, g_ffn_pre, g_ffn_post, w_peer_q, sub_keys, peer_u, peer_v):
    t, d = x2.shape
    q_rank = g_q_lat.shape[0]
    kv_rank = g_kv_lat.shape[0]
    n_mla = d // MLA_V_DIM
    kvw = SWA_KV_HEADS * SWA_HEAD_DIM
    heads = sub_keys.shape[0]

    ada = _ada(cond_in, w_ada, b_ada).reshape(batch, N_ADA, 1, d)
    shift_a, scale_a, gate_a, shift_f, scale_f, gate_f = [ada[:, i] for i in range(N_ADA)]

    w_a_cols = q_rank + kv_rank + MLA_ROPE_DIM
    w_a = jnp.pad(w_in[:, :w_a_cols], ((0, 0), (0, -w_a_cols % (2 * LANES)))).astype(BF16)
    off_swa = w_a_cols
    off_gate = off_swa + d + 2 * kvw
    w_swa = w_in[:, off_swa:off_gate].astype(BF16)
    w_g = w_in[:, off_gate:].astype(BF16)
    qk = MLA_NOPE_DIM + MLA_ROPE_DIM
    w_q = jnp.pad(w_q_b.reshape(q_rank, n_mla, qk),
                  ((0, 0), (0, 0), (0, MLA_QK_PAD - qk))).reshape(q_rank, n_mla * MLA_QK_PAD).astype(BF16)
    w_kv = w_kv_b.reshape(kv_rank, n_mla, MLA_NOPE_DIM + MLA_V_DIM)
    w_kn = w_kv[:, :, :MLA_NOPE_DIM].reshape(kv_rank, n_mla * MLA_NOPE_DIM).astype(BF16)
    w_v = w_kv[:, :, MLA_NOPE_DIM:].reshape(kv_rank, n_mla * MLA_V_DIM).astype(BF16)
    rope = _rope_tables(seq)

    h = _prenorm(x2, g_attn_pre, scale_a, shift_a, seq)
    z_a = _matmul(h, w_a, F32, name="in_proj_mla")
    z_swa = _matmul(h, w_swa, F32, name="in_proj_swa")
    z_g = _matmul(h, w_g, F32, name="in_proj_gates")
    q = _mla_q(z_a, g_q_lat, w_q, rope, seq, q_rank, n_mla)
    k, v = _mla_kv(z_a, g_kv_lat, w_kn, w_v, rope, seq, q_rank, kv_rank, n_mla)
    gated_a = _mla_attn(q, k, v, z_g, batch, seq, n_mla)
    mix = _swa(z_swa, z_g, gated_a, sinks, seq, d)
    y = _matmul(mix, w_out.astype(BF16), F32, name="out_proj")
    x1, h2t = _post_attn(x2, y, g_attn_post, gate_a, g_ffn_pre, scale_f, shift_f, seq)

    pqt = _matmul(w_peer_q.T.astype(BF16), h2t, BF16, name="peer_query")
    n1, e1, b2, e2 = _peer_topk(pqt, sub_keys.astype(BF16))
    yt = _peer_dense(h2t, peer_u.astype(BF16), peer_v.astype(BF16), n1, e1, b2, e2, heads)
    return _final(x1, yt, g_ffn_post, gate_f, seq)


def kernel(x, c, w_ada, b_ada, g_attn_pre, g_attn_post, w_in, g_q_lat, w_q_b, g_kv_lat, w_kv_b, sinks, w_out, g_ffn_pre, g_ffn_post, w_peer_q, peer_sub_keys, peer_u, peer_v):
    batch, seq, d = x.shape
    x2 = x.reshape(batch * seq, d)
    for l in range(w_ada.shape[0]):
        x2 = _layer(x2, c, batch, seq, w_ada[l], b_ada[l], g_attn_pre[l], g_attn_post[l], w_in[l],
                    g_q_lat[l], w_q_b[l], g_kv_lat[l], w_kv_b[l], sinks[l], w_out[l], g_ffn_pre[l],
                    g_ffn_post[l], w_peer_q[l], peer_sub_keys[l], peer_u[l], peer_v[l])
    return x2.reshape(batch, seq, d)
```

```python
import functools
import math

import jax
import jax.numpy as jnp
from jax import lax
from jax.experimental import pallas as pl
from jax.experimental.pallas import tpu as pltpu

F32 = jnp.float32
BF16 = jnp.bfloat16

MLA_NOPE_DIM = 128
MLA_ROPE_DIM = 64
MLA_V_DIM = 128
MLA_QK_PAD = 256
ROPE_THETA = 10000.0
SWA_HEAD_DIM = 64
SWA_KV_HEADS = 8
WINDOW = 128
PEER_HEADS = 8
PEER_N_KEYS = 128
PEER_TOPK = 16
N_ADA = 6
NORM_EPS = 1e-6
NEG_INF = -1e30

LANES = 128
VMEM_LIMIT = 56 * 1024 * 1024
VMEM_LIMIT_LARGE = 60 * 1024 * 1024


def _params(sem, vmem=VMEM_LIMIT):
    return pltpu.CompilerParams(dimension_semantics=sem, vmem_limit_bytes=vmem)


def _rms(x, g):
    return x * lax.rsqrt(jnp.mean(x * x, axis=-1, keepdims=True) + NORM_EPS) * g


def _ada_kernel(cb_ref, w_ref, b_ref, o_ref):
    nb = cb_ref.shape[0]
    tn = w_ref.shape[1]
    for b in range(nb):
        cb = cb_ref[b]
        cond = cb * jax.nn.sigmoid(cb)
        for j in range(tn // LANES):
            sl = slice(j * LANES, (j + 1) * LANES)
            r = jnp.sum(w_ref[:, sl] * cond, axis=0, keepdims=True)
            o_ref[b:b + 1, sl] = r + b_ref[:, sl]


def _ada(c, w, bias):
    nb, k = c.shape
    n = w.shape[1]
    tn = min(512, n)
    cb = jnp.broadcast_to(c[:, :, None], (nb, k, LANES))
    return pl.pallas_call(
        _ada_kernel,
        grid=(n // tn,),
        in_specs=[pl.BlockSpec((nb, k, LANES), lambda j: (0, 0, 0)),
                  pl.BlockSpec((k, tn), lambda j: (0, j)),
                  pl.BlockSpec((1, tn), lambda j: (0, j))],
        out_specs=pl.BlockSpec((nb, tn), lambda j: (0, j)),
        out_shape=jax.ShapeDtypeStruct((nb, n), F32),
        compiler_params=_params(("arbitrary",)),
        name="ada_matvec",
    )(cb, w, bias.reshape(1, n))


def _prenorm_kernel(x_ref, g_ref, sc_ref, sh_ref, o_ref):
    h = _rms(x_ref[...], g_ref[...]) * (1.0 + sc_ref[0]) + sh_ref[0]
    o_ref[...] = h.astype(o_ref.dtype)


def _prenorm(x2, g, scale, shift, seq):
    t, d = x2.shape
    tm = min(256, seq)
    per = seq // tm
    row = pl.BlockSpec((tm, d), lambda i: (i, 0))
    mod = pl.BlockSpec((1, 1, d), lambda i: (i // per, 0, 0))
    return pl.pallas_call(
        _prenorm_kernel,
        grid=(t // tm,),
        in_specs=[row, pl.BlockSpec((1, d), lambda i: (0, 0)), mod, mod],
        out_specs=row,
        out_shape=jax.ShapeDtypeStruct((t, d), BF16),
        compiler_params=_params(("arbitrary",)),
        name="prenorm_modulate",
    )(x2, g.reshape(1, d), scale, shift)


def _post_kernel(x_ref, y_ref, gpost_ref, gate_ref, gpre_ref, sc_ref, sh_ref, x1_ref, h_ref):
    x1 = x_ref[...] + gate_ref[0] * _rms(y_ref[...], gpost_ref[...])
    x1_ref[...] = x1
    h = _rms(x1, gpre_ref[...]) * (1.0 + sc_ref[0]) + sh_ref[0]
    h_ref[...] = h.T.astype(h_ref.dtype)


def _post_attn(x2, y, g_post, gate, g_pre, scale, shift, seq):
    t, d = x2.shape
    tm = min(256, seq)
    per = seq // tm
    row = pl.BlockSpec((tm, d), lambda i: (i, 0))
    vec = pl.BlockSpec((1, d), lambda i: (0, 0))
    mod = pl.BlockSpec((1, 1, d), lambda i: (i // per, 0, 0))
    return pl.pallas_call(
        _post_kernel,
        grid=(t // tm,),
        in_specs=[row, row, vec, mod, vec, mod, mod],
        out_specs=[row, pl.BlockSpec((d, tm), lambda i: (0, i))],
        out_shape=[jax.ShapeDtypeStruct((t, d), F32), jax.ShapeDtypeStruct((d, t), BF16)],
        compiler_params=_params(("arbitrary",)),
        name="post_attn_norm",
    )(x2, y, g_post.reshape(1, d), gate, g_pre.reshape(1, d), scale, shift)


def _final_kernel(x_ref, yt_ref, g_ref, gate_ref, o_ref):
    o_ref[...] = x_ref[...] + gate_ref[0] * _rms(yt_ref[...].T, g_ref[...])


def _final(x1, yt, g, gate, seq):
    t, d = x1.shape
    tm = min(256, seq)
    per = seq // tm
    row = pl.BlockSpec((tm, d), lambda i: (i, 0))
    return pl.pallas_call(
        _final_kernel,
        grid=(t // tm,),
        in_specs=[row, pl.BlockSpec((d, tm), lambda i: (0, i)), pl.BlockSpec((1, d), lambda i: (0, 0)),
                  pl.BlockSpec((1, 1, d), lambda i: (i // per, 0, 0))],
        out_specs=row,
        out_shape=jax.ShapeDtypeStruct((t, d), F32),
        compiler_params=_params(("arbitrary",)),
        name="final_residual",
    )(x1, yt, g.reshape(1, d), gate)


def _mm_kernel(a_ref, w_ref, o_ref):
    o_ref[...] = jnp.dot(a_ref[...], w_ref[...], preferred_element_type=F32).astype(o_ref.dtype)


def _matmul(a, w, out_dtype, name):
    m, k = a.shape
    n = w.shape[1]
    tm = min(1024, m)
    tn = next(c for c in (1024, 896, 768, 640, 512, 384, 256, 128, n) if n % c == 0)
    return pl.pallas_call(
        _mm_kernel,
        grid=(m // tm, n // tn),
        in_specs=[pl.BlockSpec((tm, k), lambda i, j: (i, 0)),
                  pl.BlockSpec((k, tn), lambda i, j: (0, j))],
        out_specs=pl.BlockSpec((tm, tn), lambda i, j: (i, j)),
        out_shape=jax.ShapeDtypeStruct((m, n), out_dtype),
        compiler_params=_params(("arbitrary", "arbitrary")),
        name=name,
    )(a, w)


def _rope_upper(hi, cos_ref, sin_hi_ref, sin_lo_ref):
    return (hi * cos_ref[...]
            + pltpu.roll(hi, 32, axis=1) * sin_hi_ref[...]
            + pltpu.roll(hi, 96, axis=1) * sin_lo_ref[...])


def _mla_q_kernel(z_ref, g_ref, w_ref, cos_ref, s1_ref, s2_ref, o_ref, *, scale, heads):
    half = z_ref.shape[0] // 2
    parts = [slice(0, half), slice(half, 2 * half)]
    qns = [_rms(z_ref[r, :], g_ref[...]).astype(BF16) for r in parts]
    ys = [jnp.dot(qn, w_ref[...], preferred_element_type=F32) for qn in qns]
    for r, y in zip(parts, ys):
        tabs = [cos_ref.at[r, :], s1_ref.at[r, :], s2_ref.at[r, :]]
        for h in range(heads):
            lo = y[:, h * MLA_QK_PAD:h * MLA_QK_PAD + LANES]
            hi = y[:, h * MLA_QK_PAD + LANES:(h + 1) * MLA_QK_PAD]
            o_ref[h * MLA_QK_PAD:h * MLA_QK_PAD + LANES, r] = (lo * scale).T.astype(o_ref.dtype)
            o_ref[h * MLA_QK_PAD + LANES:(h + 1) * MLA_QK_PAD, r] = (
                _rope_upper(hi, *tabs) * scale).T.astype(o_ref.dtype)


def _mla_q(z_a, g_q, w_q, rope, seq, q_rank, n_heads):
    t = z_a.shape[0]
    tm = min(1024, seq)
    hb = min(4, n_heads)
    per = seq // tm
    scale = (MLA_NOPE_DIM + MLA_ROPE_DIM) ** -0.5 * math.log2(math.e)
    tab = pl.BlockSpec((tm, LANES), lambda i, j: (i % per, 0))
    return pl.pallas_call(
        functools.partial(_mla_q_kernel, scale=scale, heads=hb),
        grid=(t // tm, n_heads // hb),
        in_specs=[pl.BlockSpec((tm, q_rank), lambda i, j: (i, 0)),
                  pl.BlockSpec((1, q_rank), lambda i, j: (0, 0)),
                  pl.BlockSpec((q_rank, hb * MLA_QK_PAD), lambda i, j: (0, j)),
                  tab, tab, tab],
        out_specs=pl.BlockSpec((hb * MLA_QK_PAD, tm), lambda i, j: (j, i)),
        out_shape=jax.ShapeDtypeStruct((n_heads * MLA_QK_PAD, t), BF16),
        compiler_params=_params(("arbitrary", "arbitrary")),
        name="mla_q_proj",
    )(z_a, g_q.reshape(1, q_rank), w_q, *rope)


def _mla_kv_kernel(c_ref, pe_ref, g_ref, wk_ref, wv_ref, cos_ref, s1_ref, s2_ref,
                   k_ref, v_ref, *, heads):
    c = _rms(c_ref[...], g_ref[...]).astype(BF16)
    kn = jnp.dot(c, wk_ref[...], preferred_element_type=F32)
    v_ref[...] = jnp.dot(c, wv_ref[...], preferred_element_type=F32).T.astype(v_ref.dtype)
    pe = _rope_upper(pe_ref[...], cos_ref, s1_ref, s2_ref).astype(k_ref.dtype)
    for h in range(heads):
        k_ref[:, h * MLA_QK_PAD:h * MLA_QK_PAD + LANES] = (
            kn[:, h * LANES:(h + 1) * LANES].astype(k_ref.dtype))
        k_ref[:, h * MLA_QK_PAD + LANES:(h + 1) * MLA_QK_PAD] = pe


def _mla_kv(z_a, g_kv, w_kn, w_v, rope, seq, q_rank, kv_rank, n_heads):
    t = z_a.shape[0]
    tm = min(256, seq)
    per = seq // tm
    tk = _attn_block(seq)
    sub = tk // tm
    tab = pl.BlockSpec((tm, LANES), lambda i: (i % per, 0))
    full = lambda shape: pl.BlockSpec(shape, lambda i: (0, 0))
    return pl.pallas_call(
        functools.partial(_mla_kv_kernel, heads=n_heads),
        grid=(t // tm,),
        in_specs=[pl.BlockSpec((tm, kv_rank), lambda i: (i, q_rank // kv_rank)),
                  pl.BlockSpec((tm, LANES), lambda i: (i, (q_rank + kv_rank) // LANES)),
                  full((1, kv_rank)),
                  full((kv_rank, n_heads * MLA_NOPE_DIM)),
                  full((kv_rank, n_heads * MLA_V_DIM)),
                  tab, tab, tab],
        out_specs=[pl.BlockSpec((tm, n_heads * MLA_QK_PAD), lambda i: (i, 0)),
                   pl.BlockSpec((None, n_heads * MLA_V_DIM, tm), lambda i: (i // sub, 0, i % sub))],
        out_shape=[jax.ShapeDtypeStruct((t, n_heads * MLA_QK_PAD), BF16),
                   jax.ShapeDtypeStruct((t // tk, n_heads * MLA_V_DIM, tk), BF16)],
        compiler_params=_params(("arbitrary",)),
        name="mla_kv_proj",
    )(z_a, z_a, g_kv.reshape(1, kv_rank), w_kn, w_v, *rope)


def _attn_block(seq):
    return min(1024, seq)


def _mla_attn_kernel(qt_ref, k_ref, vt_ref, za_ref, o_ref, s_ref, m_ref, l_ref, acc_ref, *, tq, nq):
    def put_scores(slot, qi, j):
        kb = k_ref[pl.ds(pl.multiple_of(j * tq, tq), tq), :]
        s_ref[slot] = jnp.dot(kb, qt_ref[:, qi * tq:(qi + 1) * tq],
                              preferred_element_type=F32)

    def update(s, j):
        m = m_ref[...]
        m_new = jnp.maximum(m, jnp.max(s, axis=0, keepdims=True))
        alpha = jnp.exp2(m - m_new)
        p = jnp.exp2(s - m_new)
        acc_ref[...] = alpha * acc_ref[...] + jnp.dot(vt_ref[j], p.astype(BF16),
                                                      preferred_element_type=F32)
        l_ref[...] = alpha * l_ref[...] + jnp.sum(p, axis=0, keepdims=True)
        m_ref[...] = m_new

    def finish(slot, qi):
        key = lax.broadcasted_iota(jnp.int32, (tq, tq), 0)
        qry = lax.broadcasted_iota(jnp.int32, (tq, tq), 1)
        update(jnp.where(key <= qry, s_ref[slot], NEG_INF), qi)
        rows = slice(qi * tq, (qi + 1) * tq)
        o = (acc_ref[...] / l_ref[...]).T
        o_ref[rows, :] = (o * jax.nn.sigmoid(za_ref[rows, :])).astype(o_ref.dtype)

    put_scores(0, 0, 0)
    first = 0
    for qi in range(nq):
        a, b = first, 1 - first
        m_ref[...] = jnp.full(m_ref.shape, NEG_INF, F32)
        l_ref[...] = jnp.zeros(l_ref.shape, F32)
        acc_ref[...] = jnp.zeros(acc_ref.shape, F32)

        def pair(jj, _, qi=qi, a=a, b=b):
            j = 2 * jj
            put_scores(b, qi, j + 1)
            update(s_ref[a], j)
            put_scores(a, qi, j + 2)
            update(s_ref[b], j + 1)
            return 0

        if qi >= 2:
            lax.fori_loop(0, qi // 2, pair, 0)
        if qi % 2 == 1:
            put_scores(b, qi, qi)
            update(s_ref[a], qi - 1)
            diag, free = b, a
        else:
            diag, free = a, b
        if qi + 1 < nq:
            put_scores(free, qi + 1, 0)
        finish(diag, qi)
        first = free


def _mla_attn(qt, k, vt, z_g, batch, seq, n_heads):
    t = k.shape[0]
    tq = _attn_block(seq)
    nq = seq // tq
    return pl.pallas_call(
        functools.partial(_mla_attn_kernel, tq=tq, nq=nq),
        grid=(batch, n_heads),
        in_specs=[pl.BlockSpec((MLA_QK_PAD, seq), lambda b, h: (h, b)),
                  pl.BlockSpec((seq, MLA_QK_PAD), lambda b, h: (b, h)),
                  pl.BlockSpec((nq, MLA_V_DIM, tq), lambda b, h: (b, h, 0)),
                  pl.BlockSpec((seq, MLA_V_DIM), lambda b, h: (b, h))],
        out_specs=pl.BlockSpec((seq, MLA_V_DIM), lambda b, h: (b, h)),
        out_shape=jax.ShapeDtypeStruct((t, n_heads * MLA_V_DIM), BF16),
        scratch_shapes=[pltpu.VMEM((2, tq, tq), F32), pltpu.VMEM((1, tq), F32),
                        pltpu.VMEM((1, tq), F32), pltpu.VMEM((MLA_V_DIM, tq), F32)],
        compiler_params=_params(("arbitrary", "arbitrary")),
        name="mla_flash_attention",
    )(qt, k, vt, z_g)


def _swa_bias(n_heads):
    key = jnp.arange(2 * WINDOW)[:, None]
    qry = jnp.arange(WINDOW)[None, :]
    dist = qry + WINDOW - key
    band = (dist >= 0) & (dist < WINDOW)
    slopes = jnp.asarray([2.0 ** (-8.0 * (h + 1) / n_heads) * math.log2(math.e)
                          for h in range(n_heads)], F32)
    bias = jnp.where(band, -slopes[:, None, None] * dist.astype(F32), NEG_INF)
    return jnp.stack([bias, jnp.where(key >= WINDOW, bias, NEG_INF)])


def _swa_kernel(sink_ref, bias_ref, q_ref, kp_ref, kc_ref, vp_ref, vc_ref, zb_ref, a_ref, o_ref,
                *, n_heads):
    group = n_heads // SWA_KV_HEADS
    hd = SWA_HEAD_DIM
    log2e = math.log2(math.e)
    scale = hd ** -0.5 * log2e
    nt = (((1,), (1,)), ((), ()))
    tn = (((0,), (0,)), ((), ()))
    for n in range(SWA_KV_HEADS):
        ksl = slice(n * hd, (n + 1) * hd)
        kb = jnp.concatenate([kp_ref[:, ksl], kc_ref[:, ksl]], axis=0).astype(BF16)
        vb = jnp.concatenate([vp_ref[:, ksl], vc_ref[:, ksl]], axis=0).astype(BF16)
        heads = range(n * group, (n + 1) * group)
        qs = [(q_ref[:, hq * hd:(hq + 1) * hd] * scale).astype(BF16) for hq in heads]
        ss = [lax.dot_general(kb, qh, nt, preferred_element_type=F32) for qh in qs]
        ps, dens = [], []
        for hq, s in zip(heads, ss):
            sink = sink_ref[hq] * log2e
            s = s + bias_ref[hq]
            m = jnp.maximum(jnp.max(s, axis=0, keepdims=True), sink)
            p = jnp.exp2(s - m)
            dens.append(jnp.sum(p, axis=0, keepdims=True) + jnp.exp2(sink - m))
            ps.append(p.astype(BF16))
        outs = [lax.dot_general(vb, p, tn, preferred_element_type=F32) / den
                for p, den in zip(ps, dens)]
        for g in range(0, group, 2):
            hq = n * group + g
            sl = slice(hq * hd, (hq + 2) * hd)
            o2 = jnp.concatenate([outs[g], outs[g + 1]], axis=0).T
            mix = a_ref[:, sl].astype(F32) + jax.nn.sigmoid(zb_ref[:, sl]) * o2
            o_ref[:, sl] = mix.astype(o_ref.dtype)


def _swa(z_swa, z_g, gated_a, sinks, seq, d):
    t = z_swa.shape[0]
    n_heads = d // SWA_HEAD_DIM
    kvw = SWA_KV_HEADS * SWA_HEAD_DIM
    per = seq // WINDOW
    kcol = d // kvw
    prev = lambda c: (lambda i: (jnp.maximum(i - 1, 0), c))
    cur = lambda c: (lambda i: (i, c))
    return pl.pallas_call(
        functools.partial(_swa_kernel, n_heads=n_heads),
        grid=(t // WINDOW,),
        in_specs=[pl.BlockSpec(memory_space=pltpu.SMEM),
                  pl.BlockSpec((None, n_heads, 2 * WINDOW, WINDOW),
                               lambda i: (jnp.where(i % per == 0, 1, 0), 0, 0, 0)),
                  pl.BlockSpec((WINDOW, d), cur(0)),
                  pl.BlockSpec((WINDOW, kvw), prev(kcol)),
                  pl.BlockSpec((WINDOW, kvw), cur(kcol)),
                  pl.BlockSpec((WINDOW, kvw), prev(kcol + 1)),
                  pl.BlockSpec((WINDOW, kvw), cur(kcol + 1)),
                  pl.BlockSpec((WINDOW, d), cur(1)),
                  pl.BlockSpec((WINDOW, d), cur(0))],
        out_specs=pl.BlockSpec((WINDOW, d), cur(0)),
        out_shape=jax.ShapeDtypeStruct((t, d), BF16),
        compiler_params=_params(("arbitrary",)),
        name="swa_sink_attention",
    )(sinks, _swa_bias(n_heads), z_swa, z_swa, z_swa, z_swa, z_swa, z_g, gated_a)


def _top16(ss, exact):
    ss = list(ss)
    rows = lax.broadcasted_iota(jnp.int32, ss[0].shape, 0).astype(F32)
    ranks = [jnp.full(s.shape, float(PEER_TOPK), F32) for s in ss]
    vals = [[] for _ in ss]
    for r in range(PEER_TOPK):
        for i, s in enumerate(ss):
            m = jnp.max(s, axis=0, keepdims=True)
            if exact:
                first = jnp.min(jnp.where(s == m, rows, float(PEER_N_KEYS)), axis=0, keepdims=True)
                hit = rows == first
            else:
                hit = s == m
            ranks[i] = jnp.where(hit, float(r), ranks[i])
            ss[i] = jnp.where(hit, NEG_INF, s)
            vals[i].append(m)
    return list(zip(vals, ranks))


def _peer_select(s1, s2, exact):
    k = PEER_TOPK
    (v1, rank1), (v2, rank2) = _top16([s1, s2], exact)
    n = s1.shape[1]
    hk = k // 2
    r8 = lax.broadcasted_iota(jnp.int32, (hk, n), 0).astype(F32)
    colsum = lambda x: jnp.sum(x, axis=0, keepdims=True)

    def stack(vals):
        out = jnp.zeros((hk, n), F32)
        for i, v in enumerate(vals):
            out = jnp.where(r8 == float(i), v, out)
        return out

    v2_lo, v2_hi, v1_hi = stack(v2[:hk]), stack(v2[hk:]), stack(v1[hk:])
    cand = [v1[0] + v2_lo, v1[0] + v2_hi]
    code = [r8, r8 + float(hk)]
    for a in range(1, hk):
        cand.append(jnp.where(r8 < float(k // (a + 1)), v1[a] + v2_lo, NEG_INF))
        code.append(r8 + float(a * k))
    cand.append(v1_hi + v2[0])
    code.append((r8 + float(hk)) * float(k))
    sel = [jnp.zeros((hk, n), F32) for _ in cand]
    for _ in range(k):
        m = jnp.max(functools.reduce(jnp.maximum, cand), axis=0, keepdims=True)
        if exact:
            first = functools.reduce(
                jnp.minimum, [jnp.where(cv == m, cd, float(k * k)) for cv, cd in zip(cand, code)])
            first = jnp.min(first, axis=0, keepdims=True)
        for i in range(len(cand)):
            hit = (code[i] == first) if exact else (cand[i] == m)
            sel[i] = jnp.where(hit, 1.0, sel[i])
            cand[i] = jnp.where(hit, NEG_INF, cand[i])
    e2_lo = jnp.exp(v2_lo - v2[0])
    e2_hi = jnp.exp(v2_hi - v2[0])
    cnt = [colsum(sel[0]) + colsum(sel[1])] + [colsum(sel[a + 1]) for a in range(1, hk)]
    mass = [colsum(sel[0] * e2_lo) + colsum(sel[1] * e2_hi)]
    mass += [colsum(sel[a + 1] * e2_lo) for a in range(1, hk)]
    z = jnp.zeros((1, n), F32)
    n1 = jnp.zeros(s1.shape, F32)
    for a in range(hk):
        z = z + jnp.exp(v1[a] - v1[0]) * mass[a]
        n1 = jnp.where(rank1 == float(a), cnt[a], n1)
    z = z + colsum(sel[-1] * jnp.exp(v1_hi - v1[0]))
    for a in range(hk, k):
        n1 = jnp.where(rank1 == float(a), sel[-1][a - hk:a - hk + 1, :], n1)
    e1 = jnp.exp(s1 - v1[0])
    e2 = jnp.exp(s2 - v2[0]) / z
    if exact:
        bad = jnp.zeros((1, n), F32)
    else:
        ranked = lambda rk: colsum(jnp.where(rk < float(k), 1.0, 0.0))
        picked = functools.reduce(lambda x, y: x + y, cnt) + colsum(sel[-1])
        bad = (jnp.abs(ranked(rank1) - float(k)) + jnp.abs(ranked(rank2) - float(k))
               + jnp.abs(picked - float(k)))
    return (n1, e1, rank2, e2), bad


def _peer_topk_kernel(q_ref, key_ref, n1_ref, e1_ref, b2_ref, e2_ref, *, sub):
    tm = q_ref.shape[1]
    half = q_ref.shape[0] // 2

    def scores(cs):
        s1 = jnp.dot(key_ref[0, 0], q_ref[:half, cs], preferred_element_type=F32)
        s2 = jnp.dot(key_ref[0, 1], q_ref[half:, cs], preferred_element_type=F32)
        return s1, s2

    def store(cs, res):
        n1, e1, b2, e2 = res
        n1_ref[:, cs] = n1
        e1_ref[:, cs] = e1
        b2_ref[:, cs] = b2.astype(b2_ref.dtype)
        e2_ref[:, cs] = e2.astype(e2_ref.dtype)

    groups = [slice(c * sub, (c + 1) * sub) for c in range(tm // sub)]
    bad = None
    for cs in groups:
        res, flag = _peer_select(*scores(cs), exact=False)
        store(cs, res)
        bad = flag if bad is None else jnp.maximum(bad, flag)

    @pl.when(jnp.max(bad) > 0.0)
    def _():
        for cs in groups:
            store(cs, _peer_select(*scores(cs), exact=True)[0])


def _peer_topk(qt, keys):
    t = qt.shape[1]
    heads = keys.shape[0]
    qd = qt.shape[0] // heads
    tm = min(512, t)
    out = pl.BlockSpec((PEER_N_KEYS, tm), lambda i, h: (h, i))
    shape = lambda dt: jax.ShapeDtypeStruct((heads * PEER_N_KEYS, t), dt)
    return pl.pallas_call(
        functools.partial(_peer_topk_kernel, sub=min(256, tm)),
        grid=(t // tm, heads),
        in_specs=[pl.BlockSpec((qd, tm), lambda i, h: (h, i)),
                  pl.BlockSpec((1, 2, PEER_N_KEYS, qd // 2), lambda i, h: (h, 0, 0, 0))],
        out_specs=[out, out, out, out],
        out_shape=[shape(F32), shape(F32), shape(BF16), shape(BF16)],
        compiler_params=_params(("arbitrary", "arbitrary")),
        name="peer_topk",
    )(qt, keys)


def _peer_dense_kernel(ht_ref, u_ref, v_ref, b2_ref, e2_ref, *rest, heads):
    n1_refs, e1_refs = rest[:heads], rest[heads:2 * heads]
    o_ref, a_ref = rest[2 * heads:]
    c = pl.program_id(1)
    tn = u_ref.shape[0]
    nk = PEER_N_KEYS

    @pl.when(c == 0)
    def _():
        o_ref[...] = jnp.zeros_like(o_ref)

    hid = jnp.dot(u_ref[...], ht_ref[...], preferred_element_type=F32)
    act = (0.5 * hid * (1.0 + lax.erf(hid * (2.0 ** -0.5)))).astype(BF16)
    for k in range(tn // nk):
        gate = None
        for h in range(heads):
            n1 = n1_refs[h][k:k + 1, :].astype(BF16)
            e1 = e1_refs[h][k:k + 1, :].astype(BF16)
            b2 = b2_ref[h * nk:(h + 1) * nk, :]
            e2 = e2_ref[h * nk:(h + 1) * nk, :]
            term = jnp.where(b2 < n1, e2, jnp.zeros_like(e2)) * e1
            gate = term if gate is None else gate + term
        a_ref[k * nk:(k + 1) * nk, :] = act[k * nk:(k + 1) * nk, :] * gate
    o_ref[...] += lax.dot_general(v_ref[...], a_ref[...], (((0,), (0,)), ((), ())),
                                  preferred_element_type=F32)


def _peer_dense(ht, u, v, n1, e1, b2, e2, heads):
    d, t = ht.shape
    n_exp = u.shape[0]
    tm = min(512, t)
    tn = min(1024, n_exp)
    rows = heads * PEER_N_KEYS
    per_chunk = tn // PEER_N_KEYS
    chunks_per_head = PEER_N_KEYS // per_chunk
    once = pl.Buffered(1)
    sel = pl.BlockSpec((rows, tm), lambda i, c: (0, i), pipeline_mode=once)
    head_rows = [pl.BlockSpec((per_chunk, tm), lambda i, c, h=h: (h * chunks_per_head + c, i))
                 for h in range(heads)]
    return pl.pallas_call(
        functools.partial(_peer_dense_kernel, heads=heads),
        grid=(t // tm, n_exp // tn),
        in_specs=[pl.BlockSpec((d, tm), lambda i, c: (0, i), pipeline_mode=once),
                  pl.BlockSpec((tn, d), lambda i, c: (c, 0)),
                  pl.BlockSpec((tn, d), lambda i, c: (c, 0)),
                  sel, sel] + head_rows + head_rows,
        out_specs=pl.BlockSpec((d, tm), lambda i, c: (0, i), pipeline_mode=once),
        out_shape=jax.ShapeDtypeStruct((d, t), F32),
        scratch_shapes=[pltpu.VMEM((tn, tm), BF16)],
        compiler_params=_params(("arbitrary", "arbitrary"), vmem=VMEM_LIMIT_LARGE),
        name="peer_dense_experts",
    )(ht, u, v, b2, e2, *([n1] * heads), *([e1] * heads))


def _rope_tables(seq):
    half = MLA_ROPE_DIM // 2
    pos = jnp.arange(seq, dtype=F32)
    inv_freq = ROPE_THETA ** (-jnp.arange(0, MLA_ROPE_DIM, 2, dtype=F32) / MLA_ROPE_DIM)
    ang = pos[:, None] * inv_freq[None, :]
    cos, sin = jnp.cos(ang), jnp.sin(ang)
    zero = jnp.zeros((seq, half), F32)
    pad = jnp.zeros((seq, LANES - MLA_ROPE_DIM), F32)
    cos_t = jnp.concatenate([cos, cos, pad], axis=1)
    sin_hi = jnp.concatenate([zero, sin, pad], axis=1)
    sin_lo = jnp.concatenate([-sin, zero, pad], axis=1)
    return cos_t, sin_hi, sin_lo


def _layer(x2, cond_in, batch, seq, w_ada, b_ada, g_attn_pre, g_attn_post, w_in, g_q_lat, w_q_b,
           g_kv_lat, w_kv_b, sinks, w_out, g_ffn_pre, g_ffn_post, w_peer_q, sub_keys, peer_u, peer_v):
    t, d = x2.shape
    q_rank = g_q_lat.shape[0]
    kv_rank = g_kv_lat.shape[0]
    n_mla = d // MLA_V_DIM
    kvw = SWA_KV_HEADS * SWA_HEAD_DIM
    heads = sub_keys.shape[0]

    ada = _ada(cond_in, w_ada, b_ada).reshape(batch, N_ADA, 1, d)
    shift_a, scale_a, gate_a, shift_f, scale_f, gate_f = [ada[:, i] for i in range(N_ADA)]

    w_a_cols = q_rank + kv_rank + MLA_ROPE_DIM
    w_a = jnp.pad(w_in[:, :w_a_cols], ((0, 0), (0, -w_a_cols % (2 * LANES)))).astype(BF16)
    off_swa = w_a_cols
    off_gate = off_swa + d + 2 * kvw
    w_swa = w_in[:, off_swa:off_gate].astype(BF16)
    w_g = w_in[:, off_gate:].astype(BF16)
    qk = MLA_NOPE_DIM + MLA_ROPE_DIM
    w_q = jnp.pad(w_q_b.reshape(q_rank, n_mla, qk),
                  ((0, 0), (0, 0), (0, MLA_QK_PAD - qk))).reshape(q_rank, n_mla * MLA_QK_PAD).astype(BF16)
    w_kv = w_kv_b.reshape(kv_rank, n_mla, MLA_NOPE_DIM + MLA_V_DIM)
    w_kn = w_kv[:, :, :MLA_NOPE_DIM].reshape(kv_rank, n_mla * MLA_NOPE_DIM).astype(BF16)
    w_v = w_kv[:, :, MLA_NOPE_DIM:].reshape(kv_rank, n_mla * MLA_V_DIM).astype(BF16)
    rope = _rope_tables(seq)

    h = _prenorm(x2, g_attn_pre, scale_a, shift_a, seq)
    z_a = _matmul(h, w_a, F32, name="in_proj_mla")
    z_swa = _matmul(h, w_swa, F32, name="in_proj_swa")
    z_g = _matmul(h, w_g, F32, name="in_proj_gates")
    q = _mla_q(z_a, g_q_lat, w_q, rope, seq, q_rank, n_mla)
    k, v = _mla_kv(z_a, g_kv_lat, w_kn, w_v, rope, seq, q_rank, kv_rank, n_mla)
    gated_a = _mla_attn(q, k, v, z_g, batch, seq, n_mla)
    mix = _swa(z_swa, z_g, gated_a, sinks, seq, d)
    y = _matmul(mix, w_out.astype(BF16), F32, name="out_proj")
    x1, h2t = _post_attn(x2, y, g_attn_post, gate_a, g_ffn_pre, scale_f, shift_f, seq)

    pqt = _matmul(w_peer_q.T.astype(BF16), h2t, BF16, name="peer_query")
    n1, e1, b2, e2 = _peer_topk(pqt, sub_keys.astype(BF16))
    yt = _peer_dense(h2t, peer_u.astype(BF16), peer_v.astype(BF16), n1, e1, b2, e2, heads)
    return _final(x1, yt, g_ffn_post, gate_f, seq)


def kernel(x, c, w_ada, b_ada, g_attn_pre, g_attn_post, w_in, g_q_lat, w_q_b, g_kv_lat, w_kv_b, sinks, w_out, g_ffn_pre, g_ffn_post, w_peer_q, peer_sub_keys, peer_u, peer_v):
    batch, seq, d = x.shape
    x2 = x.reshape(batch * seq, d)
    for l in range(w_ada.shape[0]):
        x2 = _layer(x2, c, batch, seq, w_ada[l], b_ada[l], g_attn_pre[l], g_attn_post[l], w_in[l],
                    g_q_lat[l], w_q_b[l], g_kv_lat[l], w_kv_b[l], sinks[l], w_out[l], g_ffn_pre[l],
                    g_ffn_post[l], w_peer_q[l], peer_sub_keys[l], peer_u[l], peer_v[l])
    return x2.reshape(batch, seq, d)
```

```python
import functools
import math

import jax
import jax.numpy as jnp
from jax import lax
from jax.experimental import pallas as pl
from jax.experimental.pallas import tpu as pltpu

F32 = jnp.float32
BF16 = jnp.bfloat16

MLA_NOPE_DIM = 128
MLA_ROPE_DIM = 64
MLA_V_DIM = 128
MLA_QK_PAD = 256
ROPE_THETA = 10000.0
SWA_HEAD_DIM = 64
SWA_KV_HEADS = 8
WINDOW = 128
PEER_HEADS = 8
PEER_N_KEYS = 128
PEER_TOPK = 16
N_ADA = 6
NORM_EPS = 1e-6
NEG_INF = -1e30

LANES = 128
VMEM_LIMIT = 56 * 1024 * 1024
VMEM_LIMIT_LARGE = 60 * 1024 * 1024


def _params(sem, vmem=VMEM_LIMIT):
    return pltpu.CompilerParams(dimension_semantics=sem, vmem_limit_bytes=vmem)


def _rms(x, g):
    return x * lax.rsqrt(jnp.mean(x * x, axis=-1, keepdims=True) + NORM_EPS) * g


def _ada_kernel(cb_ref, w_ref, b_ref, o_ref):
    nb = cb_ref.shape[0]
    tn = w_ref.shape[1]
    for b in range(nb):
        cb = cb_ref[b]
        cond = cb * jax.nn.sigmoid(cb)
        for j in range(tn // LANES):
            sl = slice(j * LANES, (j + 1) * LANES)
            r = jnp.sum(w_ref[:, sl] * cond, axis=0, keepdims=True)
            o_ref[b:b + 1, sl] = r + b_ref[:, sl]


def _ada(c, w, bias):
    nb, k = c.shape
    n = w.shape[1]
    tn = min(512, n)
    cb = jnp.broadcast_to(c[:, :, None], (nb, k, LANES))
    return pl.pallas_call(
        _ada_kernel,
        grid=(n // tn,),
        in_specs=[pl.BlockSpec((nb, k, LANES), lambda j: (0, 0, 0)),
                  pl.BlockSpec((k, tn), lambda j: (0, j)),
                  pl.BlockSpec((1, tn), lambda j: (0, j))],
        out_specs=pl.BlockSpec((nb, tn), lambda j: (0, j)),
        out_shape=jax.ShapeDtypeStruct((nb, n), F32),
        compiler_params=_params(("arbitrary",)),
        name="ada_matvec",
    )(cb, w, bias.reshape(1, n))


def _prenorm_kernel(x_ref, g_ref, sc_ref, sh_ref, o_ref):
    h = _rms(x_ref[...], g_ref[...]) * (1.0 + sc_ref[0]) + sh_ref[0]
    o_ref[...] = h.astype(o_ref.dtype)


def _prenorm(x2, g, scale, shift, seq):
    t, d = x2.shape
    tm = min(256, seq)
    per = seq // tm
    row = pl.BlockSpec((tm, d), lambda i: (i, 0))
    mod = pl.BlockSpec((1, 1, d), lambda i: (i // per, 0, 0))
    return pl.pallas_call(
        _prenorm_kernel,
        grid=(t // tm,),
        in_specs=[row, pl.BlockSpec((1, d), lambda i: (0, 0)), mod, mod],
        out_specs=row,
        out_shape=jax.ShapeDtypeStruct((t, d), BF16),
        compiler_params=_params(("arbitrary",)),
        name="prenorm_modulate",
    )(x2, g.reshape(1, d), scale, shift)


def _post_kernel(x_ref, y_ref, gpost_ref, gate_ref, gpre_ref, sc_ref, sh_ref, x1_ref, h_ref):
    x1 = x_ref[...] + gate_ref[0] * _rms(y_ref[...], gpost_ref[...])
    x1_ref[...] = x1
    h = _rms(x1, gpre_ref[...]) * (1.0 + sc_ref[0]) + sh_ref[0]
    h_ref[...] = h.T.astype(h_ref.dtype)


def _post_attn(x2, y, g_post, gate, g_pre, scale, shift, seq):
    t, d = x2.shape
    tm = min(256, seq)
    per = seq // tm
    row = pl.BlockSpec((tm, d), lambda i: (i, 0))
    vec = pl.BlockSpec((1, d), lambda i: (0, 0))
    mod = pl.BlockSpec((1, 1, d), lambda i: (i // per, 0, 0))
    return pl.pallas_call(
        _post_kernel,
        grid=(t // tm,),
        in_specs=[row, row, vec, mod, vec, mod, mod],
        out_specs=[row, pl.BlockSpec((d, tm), lambda i: (0, i))],
        out_shape=[jax.ShapeDtypeStruct((t, d), F32), jax.ShapeDtypeStruct((d, t), BF16)],
        compiler_params=_params(("arbitrary",)),
        name="post_attn_norm",
    )(x2, y, g_post.reshape(1, d), gate, g_pre.reshape(1, d), scale, shift)


def _final_kernel(x_ref, yt_ref, g_ref, gate_ref, o_ref):
    o_ref[...] = x_ref[...] + gate_ref[0] * _rms(yt_ref[...].T, g_ref[...])


def _final(x1, yt, g, gate, seq):
    t, d = x1.shape
    tm = min(256, seq)
    per = seq // tm
    row = pl.BlockSpec((tm, d), lambda i: (i, 0))
    return pl.pallas_call(
        _final_kernel,
        grid=(t // tm,),
        in_specs=[row, pl.BlockSpec((d, tm), lambda i: (0, i)), pl.BlockSpec((1, d), lambda i: (0, 0)),
                  pl.BlockSpec((1, 1, d), lambda i: (i // per, 0, 0))],
        out_specs=row,
        out_shape=jax.ShapeDtypeStruct((t, d), F32),
        compiler_params=_params(("arbitrary",)),
        name="final_residual",
    )(x1, yt, g.reshape(1, d), gate)


def _mm_kernel(a_ref, w_ref, o_ref):
    o_ref[...] = jnp.dot(a_ref[...], w_ref[...], preferred_element_type=F32).astype(o_ref.dtype)


def _matmul(a, w, out_dtype, name):
    m, k = a.shape
    n = w.shape[1]
    tm = min(1024, m)
    tn = next(c for c in (1024, 896, 768, 640, 512, 384, 256, 128, n) if n % c == 0)
    return pl.pallas_call(
        _mm_kernel,
        grid=(m // tm, n // tn),
        in_specs=[pl.BlockSpec((tm, k), lambda i, j: (i, 0)),
                  pl.BlockSpec((k, tn), lambda i, j: (0, j))],
        out_specs=pl.BlockSpec((tm, tn), lambda i, j: (i, j)),
        out_shape=jax.ShapeDtypeStruct((m, n), out_dtype),
        compiler_params=_params(("arbitrary", "arbitrary")),
        name=name,
    )(a, w)


def _rope_upper(hi, cos_ref, sin_hi_ref, sin_lo_ref):
    return (hi * cos_ref[...]
            + pltpu.roll(hi, 32, axis=1) * sin_hi_ref[...]
            + pltpu.roll(hi, 96, axis=1) * sin_lo_ref[...])


def _mla_q_kernel(z_ref, g_ref, w_ref, cos_ref, s1_ref, s2_ref, o_ref, *, scale, heads):
    half = z_ref.shape[0] // 2
    parts = [slice(0, half), slice(half, 2 * half)]
    qns = [_rms(z_ref[r, :], g_ref[...]).astype(BF16) for r in parts]
    ys = [jnp.dot(qn, w_ref[...], preferred_element_type=F32) for qn in qns]
    for r, y in zip(parts, ys):
        tabs = [cos_ref.at[r, :], s1_ref.at[r, :], s2_ref.at[r, :]]
        for h in range(heads):
            lo = y[:, h * MLA_QK_PAD:h * MLA_QK_PAD + LANES]
            hi = y[:, h * MLA_QK_PAD + LANES:(h + 1) * MLA_QK_PAD]
            o_ref[h * MLA_QK_PAD:h * MLA_QK_PAD + LANES, r] = (lo * scale).T.astype(o_ref.dtype)
            o_ref[h * MLA_QK_PAD + LANES:(h + 1) * MLA_QK_PAD, r] = (
                _rope_upper(hi, *tabs) * scale).T.astype(o_ref.dtype)


def _mla_q(z_a, g_q, w_q, rope, seq, q_rank, n_heads):
    t = z_a.shape[0]
    tm = min(1024, seq)
    hb = min(4, n_heads)
    per = seq // tm
    scale = (MLA_NOPE_DIM + MLA_ROPE_DIM) ** -0.5 * math.log2(math.e)
    tab = pl.BlockSpec((tm, LANES), lambda i, j: (i % per, 0))
    return pl.pallas_call(
        functools.partial(_mla_q_kernel, scale=scale, heads=hb),
        grid=(t // tm, n_heads // hb),
        in_specs=[pl.BlockSpec((tm, q_rank), lambda i, j: (i, 0)),
                  pl.BlockSpec((1, q_rank), lambda i, j: (0, 0)),
                  pl.BlockSpec((q_rank, hb * MLA_QK_PAD), lambda i, j: (0, j)),
                  tab, tab, tab],
        out_specs=pl.BlockSpec((hb * MLA_QK_PAD, tm), lambda i, j: (j, i)),
        out_shape=jax.ShapeDtypeStruct((n_heads * MLA_QK_PAD, t), BF16),
        compiler_params=_params(("arbitrary", "arbitrary")),
        name="mla_q_proj",
    )(z_a, g_q.reshape(1, q_rank), w_q, *rope)


def _mla_kv_kernel(c_ref, pe_ref, g_ref, wk_ref, wv_ref, cos_ref, s1_ref, s2_ref,
                   k_ref, v_ref, *, heads):
    c = _rms(c_ref[...], g_ref[...]).astype(BF16)
    kn = jnp.dot(c, wk_ref[...], preferred_element_type=F32)
    v_ref[...] = jnp.dot(c, wv_ref[...], preferred_element_type=F32).T.astype(v_ref.dtype)
    pe = _rope_upper(pe_ref[...], cos_ref, s1_ref, s2_ref).astype(k_ref.dtype)
    for h in range(heads):
        k_ref[:, h * MLA_QK_PAD:h * MLA_QK_PAD + LANES] = (
            kn[:, h * LANES:(h + 1) * LANES].astype(k_ref.dtype))
        k_ref[:, h * MLA_QK_PAD + LANES:(h + 1) * MLA_QK_PAD] = pe


def _mla_kv(z_a, g_kv, w_kn, w_v, rope, seq, q_rank, kv_rank, n_heads):
    t = z_a.shape[0]
    tm = min(256, seq)
    per = seq // tm
    tk = _attn_block(seq)
    sub = tk // tm
    tab = pl.BlockSpec((tm, LANES), lambda i: (i % per, 0))
    full = lambda shape: pl.BlockSpec(shape, lambda i: (0, 0))
    return pl.pallas_call(
        functools.partial(_mla_kv_kernel, heads=n_heads),
        grid=(t // tm,),
        in_specs=[pl.BlockSpec((tm, kv_rank), lambda i: (i, q_rank // kv_rank)),
                  pl.BlockSpec((tm, LANES), lambda i: (i, (q_rank + kv_rank) // LANES)),
                  full((1, kv_rank)),
                  full((kv_rank, n_heads * MLA_NOPE_DIM)),
                  full((kv_rank, n_heads * MLA_V_DIM)),
                  tab, tab, tab],
        out_specs=[pl.BlockSpec((tm, n_heads * MLA_QK_PAD), lambda i: (i, 0)),
                   pl.BlockSpec((None, n_heads * MLA_V_DIM, tm), lambda i: (i // sub, 0, i % sub))],
        out_shape=[jax.ShapeDtypeStruct((t, n_heads * MLA_QK_PAD), BF16),
                   jax.ShapeDtypeStruct((t // tk, n_heads * MLA_V_DIM, tk), BF16)],
        compiler_params=_params(("arbitrary",)),
        name="mla_kv_proj",
    )(z_a, z_a, g_kv.reshape(1, kv_rank), w_kn, w_v, *rope)


def _attn_block(seq):
    return min(1024, seq)


def _mla_attn_kernel(qt_ref, k_ref, vt_ref, za_ref, o_ref, s_ref, m_ref, l_ref, acc_ref, *, tq, nq):
    def put_scores(slot, qi, j):
        start = j * tq if isinstance(j, int) else pl.multiple_of(j * tq, tq)
        kb = k_ref[pl.ds(start, tq), :]
        s_ref[slot] = jnp.dot(kb, qt_ref[:, qi * tq:(qi + 1) * tq],
                              preferred_element_type=F32)

    def update(s, j, keys=slice(None), qs=slice(None)):
        m = m_ref[:, qs]
        m_new = jnp.maximum(m, jnp.max(s, axis=0, keepdims=True))
        alpha = jnp.exp2(m - m_new)
        p = jnp.exp2(s - m_new)
        acc_ref[:, qs] = alpha * acc_ref[:, qs] + jnp.dot(
            vt_ref[j, :, keys], p.astype(BF16), preferred_element_type=F32)
        l_ref[:, qs] = alpha * l_ref[:, qs] + jnp.sum(p, axis=0, keepdims=True)
        m_ref[:, qs] = m_new

    hq = tq // 2
    diag_parts = ((slice(0, hq), slice(0, tq)), (slice(hq, tq), slice(hq, tq)))

    def put_diag(slot, qi):
        for keys, qs in diag_parts:
            s_ref[slot, keys, qs] = jnp.dot(
                k_ref[qi * tq + keys.start:qi * tq + keys.stop, :],
                qt_ref[:, qi * tq + qs.start:qi * tq + qs.stop], preferred_element_type=F32)

    def finish(slot, qi):
        for keys, qs in diag_parts:
            shape = (hq, qs.stop - qs.start)
            key = lax.broadcasted_iota(jnp.int32, shape, 0) + (keys.start - qs.start)
            qry = lax.broadcasted_iota(jnp.int32, shape, 1)
            update(jnp.where(key <= qry, s_ref[slot, keys, qs], NEG_INF), qi, keys, qs)
        rows = slice(qi * tq, (qi + 1) * tq)
        o = (acc_ref[...] / l_ref[...]).T
        o_ref[rows, :] = (o * jax.nn.sigmoid(za_ref[rows, :])).astype(o_ref.dtype)

    put_diag(0, 0)
    first = 0
    for qi in range(nq):
        a, b = first, 1 - first
        m_ref[...] = jnp.full(m_ref.shape, NEG_INF, F32)
        l_ref[...] = jnp.zeros(l_ref.shape, F32)
        acc_ref[...] = jnp.zeros(acc_ref.shape, F32)

        def pair(jj, _, qi=qi, a=a, b=b):
            j = 2 * jj
            put_scores(b, qi, j + 1)
            update(s_ref[a], j)
            put_scores(a, qi, j + 2)
            update(s_ref[b], j + 1)
            return 0

        if qi % 2 == 1:
            if qi >= 3:
                lax.fori_loop(0, qi // 2, pair, 0)
            put_diag(b, qi)
            update(s_ref[a], qi - 1)
            diag, free = b, a
        else:
            if qi >= 4:
                lax.fori_loop(0, qi // 2 - 1, pair, 0)
            if qi >= 2:
                put_scores(b, qi, qi - 1)
                update(s_ref[a], qi - 2)
                put_diag(a, qi)
                update(s_ref[b], qi - 1)
            diag, free = a, b
        if qi + 1 < nq:
            put_scores(free, qi + 1, 0)
        finish(diag, qi)
        first = free


def _mla_attn(qt, k, vt, z_g, batch, seq, n_heads):
    t = k.shape[0]
    tq = _attn_block(seq)
    nq = seq // tq
    return pl.pallas_call(
        functools.partial(_mla_attn_kernel, tq=tq, nq=nq),
        grid=(batch, n_heads),
        in_specs=[pl.BlockSpec((MLA_QK_PAD, seq), lambda b, h: (h, b)),
                  pl.BlockSpec((seq, MLA_QK_PAD), lambda b, h: (b, h)),
                  pl.BlockSpec((nq, MLA_V_DIM, tq), lambda b, h: (b, h, 0)),
                  pl.BlockSpec((seq, MLA_V_DIM), lambda b, h: (b, h))],
        out_specs=pl.BlockSpec((seq, MLA_V_DIM), lambda b, h: (b, h)),
        out_shape=jax.ShapeDtypeStruct((t, n_heads * MLA_V_DIM), BF16),
        scratch_shapes=[pltpu.VMEM((2, tq, tq), F32), pltpu.VMEM((1, tq), F32),
                        pltpu.VMEM((1, tq), F32), pltpu.VMEM((MLA_V_DIM, tq), F32)],
        compiler_params=_params(("arbitrary", "arbitrary")),
        name="mla_flash_attention",
    )(qt, k, vt, z_g)


def _swa_bias(n_heads):
    key = jnp.arange(2 * WINDOW)[:, None]
    qry = jnp.arange(WINDOW)[None, :]
    dist = qry + WINDOW - key
    band = (dist >= 0) & (dist < WINDOW)
    slopes = jnp.asarray([2.0 ** (-8.0 * (h + 1) / n_heads) * math.log2(math.e)
                          for h in range(n_heads)], F32)
    bias = jnp.where(band, -slopes[:, None, None] * dist.astype(F32), NEG_INF)
    return jnp.stack([bias, jnp.where(key >= WINDOW, bias, NEG_INF)])


def _swa_kernel(sink_ref, bias_ref, q_ref, kp_ref, kc_ref, vp_ref, vc_ref, zb_ref, a_ref, o_ref,
                *, n_heads):
    group = n_heads // SWA_KV_HEADS
    hd = SWA_HEAD_DIM
    log2e = math.log2(math.e)
    scale = hd ** -0.5 * log2e
    nt = (((1,), (1,)), ((), ()))
    tn = (((0,), (0,)), ((), ()))
    for n in range(SWA_KV_HEADS):
        ksl = slice(n * hd, (n + 1) * hd)
        kb = jnp.concatenate([kp_ref[:, ksl], kc_ref[:, ksl]], axis=0).astype(BF16)
        vb = jnp.concatenate([vp_ref[:, ksl], vc_ref[:, ksl]], axis=0).astype(BF16)
        heads = range(n * group, (n + 1) * group)
        qs = [(q_ref[:, hq * hd:(hq + 1) * hd] * scale).astype(BF16) for hq in heads]
        ss = [lax.dot_general(kb, qh, nt, preferred_element_type=F32) for qh in qs]
        ps, dens = [], []
        for hq, s in zip(heads, ss):
            sink = sink_ref[hq] * log2e
            s = s + bias_ref[hq]
            m = jnp.maximum(jnp.max(s, axis=0, keepdims=True), sink)
            p = jnp.exp2(s - m)
            dens.append(jnp.sum(p, axis=0, keepdims=True) + jnp.exp2(sink - m))
            ps.append(p.astype(BF16))
        outs = [lax.dot_general(vb, p, tn, preferred_element_type=F32) / den
                for p, den in zip(ps, dens)]
        for g in range(0, group, 2):
            hq = n * group + g
            sl = slice(hq * hd, (hq + 2) * hd)
            o2 = jnp.concatenate([outs[g], outs[g + 1]], axis=0).T
            mix = a_ref[:, sl].astype(F32) + jax.nn.sigmoid(zb_ref[:, sl]) * o2
            o_ref[:, sl] = mix.astype(o_ref.dtype)


def _swa(z_swa, z_g, gated_a, sinks, seq, d):
    t = z_swa.shape[0]
    n_heads = d // SWA_HEAD_DIM
    kvw = SWA_KV_HEADS * SWA_HEAD_DIM
    per = seq // WINDOW
    kcol = d // kvw
    prev = lambda c: (lambda i: (jnp.maximum(i - 1, 0), c))
    cur = lambda c: (lambda i: (i, c))
    return pl.pallas_call(
        functools.partial(_swa_kernel, n_heads=n_heads),
        grid=(t // WINDOW,),
        in_specs=[pl.BlockSpec(memory_space=pltpu.SMEM),
                  pl.BlockSpec((None, n_heads, 2 * WINDOW, WINDOW),
                               lambda i: (jnp.where(i % per == 0, 1, 0), 0, 0, 0)),
                  pl.BlockSpec((WINDOW, d), cur(0)),
                  pl.BlockSpec((WINDOW, kvw), prev(kcol)),
                  pl.BlockSpec((WINDOW, kvw), cur(kcol)),
                  pl.BlockSpec((WINDOW, kvw), prev(kcol + 1)),
                  pl.BlockSpec((WINDOW, kvw), cur(kcol + 1)),
                  pl.BlockSpec((WINDOW, d), cur(1)),
                  pl.BlockSpec((WINDOW, d), cur(0))],
        out_specs=pl.BlockSpec((WINDOW, d), cur(0)),
        out_shape=jax.ShapeDtypeStruct((t, d), BF16),
        compiler_params=_params(("arbitrary",)),
        name="swa_sink_attention",
    )(sinks, _swa_bias(n_heads), z_swa, z_swa, z_swa, z_swa, z_swa, z_g, gated_a)


def _top16(ss, exact):
    ss = list(ss)
    rows = lax.broadcasted_iota(jnp.int32, ss[0].shape, 0).astype(F32)
    ranks = [jnp.full(s.shape, float(PEER_TOPK), F32) for s in ss]
    vals = [[] for _ in ss]
    for r in range(PEER_TOPK):
        for i, s in enumerate(ss):
            m = jnp.max(s, axis=0, keepdims=True)
            if exact:
                first = jnp.min(jnp.where(s == m, rows, float(PEER_N_KEYS)), axis=0, keepdims=True)
                hit = rows == first
            else:
                hit = s == m
            ranks[i] = jnp.where(hit, float(r), ranks[i])
            ss[i] = jnp.where(hit, NEG_INF, s)
            vals[i].append(m)
    return list(zip(vals, ranks))


def _peer_select(s1, s2, exact):
    k = PEER_TOPK
    (v1, rank1), (v2, rank2) = _top16([s1, s2], exact)
    n = s1.shape[1]
    hk = k // 2
    r8 = lax.broadcasted_iota(jnp.int32, (hk, n), 0).astype(F32)
    colsum = lambda x: jnp.sum(x, axis=0, keepdims=True)

    def stack(vals):
        out = jnp.zeros((hk, n), F32)
        for i, v in enumerate(vals):
            out = jnp.where(r8 == float(i), v, out)
        return out

    v2_lo, v2_hi, v1_hi = stack(v2[:hk]), stack(v2[hk:]), stack(v1[hk:])
    cand = [v1[0] + v2_lo, v1[0] + v2_hi]
    code = [r8, r8 + float(hk)]
    for a in range(1, hk):
        cand.append(jnp.where(r8 < float(k // (a + 1)), v1[a] + v2_lo, NEG_INF))
        code.append(r8 + float(a * k))
    cand.append(v1_hi + v2[0])
    code.append((r8 + float(hk)) * float(k))
    sel = [jnp.zeros((hk, n), F32) for _ in cand]
    for _ in range(k):
        m = jnp.max(functools.reduce(jnp.maximum, cand), axis=0, keepdims=True)
        if exact:
            first = functools.reduce(
                jnp.minimum, [jnp.where(cv == m, cd, float(k * k)) for cv, cd in zip(cand, code)])
            first = jnp.min(first, axis=0, keepdims=True)
        for i in range(len(cand)):
            hit = (code[i] == first) if exact else (cand[i] == m)
            sel[i] = jnp.where(hit, 1.0, sel[i])
            cand[i] = jnp.where(hit, NEG_INF, cand[i])
    e2_lo = jnp.exp(v2_lo - v2[0])
    e2_hi = jnp.exp(v2_hi - v2[0])
    cnt = [colsum(sel[0]) + colsum(sel[1])] + [colsum(sel[a + 1]) for a in range(1, hk)]
    mass = [colsum(sel[0] * e2_lo) + colsum(sel[1] * e2_hi)]
    mass += [colsum(sel[a + 1] * e2_lo) for a in range(1, hk)]
    z = jnp.zeros((1, n), F32)
    n1 = jnp.zeros(s1.shape, F32)
    for a in range(hk):
        z = z + jnp.exp(v1[a] - v1[0]) * mass[a]
        n1 = jnp.where(rank1 == float(a), cnt[a], n1)
    z = z + colsum(sel[-1] * jnp.exp(v1_hi - v1[0]))
    for a in range(hk, k):
        n1 = jnp.where(rank1 == float(a), sel[-1][a - hk:a - hk + 1, :], n1)
    e1 = jnp.exp(s1 - v1[0])
    e2 = jnp.exp(s2 - v2[0]) / z
    if exact:
        bad = jnp.zeros((1, n), F32)
    else:
        ranked = lambda rk: colsum(jnp.where(rk < float(k), 1.0, 0.0))
        picked = functools.reduce(lambda x, y: x + y, cnt) + colsum(sel[-1])
        bad = (jnp.abs(ranked(rank1) - float(k)) + jnp.abs(ranked(rank2) - float(k))
               + jnp.abs(picked - float(k)))
    return (n1, e1, rank2, e2), bad


def _peer_topk_kernel(q_ref, key_ref, n1_ref, e1_ref, b2_ref, e2_ref, *, sub):
    tm = q_ref.shape[1]
    half = q_ref.shape[0] // 2

    def scores(cs):
        s1 = jnp.dot(key_ref[0, 0], q_ref[:half, cs], preferred_element_type=F32)
        s2 = jnp.dot(key_ref[0, 1], q_ref[half:, cs], preferred_element_type=F32)
        return s1, s2

    def store(cs, res):
        n1, e1, b2, e2 = res
        n1_ref[:, cs] = n1
        e1_ref[:, cs] = e1
        b2_ref[:, cs] = b2.astype(b2_ref.dtype)
        e2_ref[:, cs] = e2.astype(e2_ref.dtype)

    groups = [slice(c * sub, (c + 1) * sub) for c in range(tm // sub)]
    bad = None
    for cs in groups:
        res, flag = _peer_select(*scores(cs), exact=False)
        store(cs, res)
        bad = flag if bad is None else jnp.maximum(bad, flag)

    @pl.when(jnp.max(bad) > 0.0)
    def _():
        for cs in groups:
            store(cs, _peer_select(*scores(cs), exact=True)[0])


def _peer_topk(qt, keys):
    t = qt.shape[1]
    heads = keys.shape[0]
    qd = qt.shape[0] // heads
    tm = min(512, t)
    out = pl.BlockSpec((PEER_N_KEYS, tm), lambda i, h: (h, i))
    shape = lambda dt: jax.ShapeDtypeStruct((heads * PEER_N_KEYS, t), dt)
    return pl.pallas_call(
        functools.partial(_peer_topk_kernel, sub=min(256, tm)),
        grid=(t // tm, heads),
        in_specs=[pl.BlockSpec((qd, tm), lambda i, h: (h, i)),
                  pl.BlockSpec((1, 2, PEER_N_KEYS, qd // 2), lambda i, h: (h, 0, 0, 0))],
        out_specs=[out, out, out, out],
        out_shape=[shape(F32), shape(F32), shape(BF16), shape(BF16)],
        compiler_params=_params(("arbitrary", "arbitrary")),
        name="peer_topk",
    )(qt, keys)


def _peer_dense_kernel(ht_ref, u_ref, v_ref, b2_ref, e2_ref, *rest, heads):
    n1_refs, e1_refs = rest[:heads], rest[heads:2 * heads]
    o_ref, a_ref = rest[2 * heads:]
    c = pl.program_id(1)
    tn = u_ref.shape[0]
    nk = PEER_N_KEYS

    @pl.when(c == 0)
    def _():
        o_ref[...] = jnp.zeros_like(o_ref)

    hid = jnp.dot(u_ref[...], ht_ref[...], preferred_element_type=F32)
    act = (0.5 * hid * (1.0 + lax.erf(hid * (2.0 ** -0.5)))).astype(BF16)
    for k in range(tn // nk):
        gate = None
        for h in range(heads):
            n1 = n1_refs[h][k:k + 1, :].astype(BF16)
            e1 = e1_refs[h][k:k + 1, :].astype(BF16)
            b2 = b2_ref[h * nk:(h + 1) * nk, :]
            e2 = e2_ref[h * nk:(h + 1) * nk, :]
            term = jnp.where(b2 < n1, e2, jnp.zeros_like(e2)) * e1
            gate = term if gate is None else gate + term
        a_ref[k * nk:(k + 1) * nk, :] = act[k * nk:(k + 1) * nk, :] * gate
    o_ref[...] += lax.dot_general(v_ref[...], a_ref[...], (((0,), (0,)), ((), ())),
                                  preferred_element_type=F32)


def _peer_dense(ht, u, v, n1, e1, b2, e2, heads):
    d, t = ht.shape
    n_exp = u.shape[0]
    tm = min(512, t)
    tn = min(1024, n_exp)
    rows = heads * PEER_N_KEYS
    per_chunk = tn // PEER_N_KEYS
    chunks_per_head = PEER_N_KEYS // per_chunk
    once = pl.Buffered(1)
    sel = pl.BlockSpec((rows, tm), lambda i, c: (0, i), pipeline_mode=once)
    head_rows = [pl.BlockSpec((per_chunk, tm), lambda i, c, h=h: (h * chunks_per_head + c, i))
                 for h in range(heads)]
    return pl.pallas_call(
        functools.partial(_peer_dense_kernel, heads=heads),
        grid=(t // tm, n_exp // tn),
        in_specs=[pl.BlockSpec((d, tm), lambda i, c: (0, i), pipeline_mode=once),
                  pl.BlockSpec((tn, d), lambda i, c: (c, 0)),
                  pl.BlockSpec((tn, d), lambda i, c: (c, 0)),
                  sel, sel] + head_rows + head_rows,
        out_specs=pl.BlockSpec((d, tm), lambda i, c: (0, i), pipeline_mode=once),
        out_shape=jax.ShapeDtypeStruct((d, t), F32),
        scratch_shapes=[pltpu.VMEM((tn, tm), BF16)],
        compiler_params=_params(("arbitrary", "arbitrary"), vmem=VMEM_LIMIT_LARGE),
        name="peer_dense_experts",
    )(ht, u, v, b2, e2, *([n1] * heads), *([e1] * heads))


def _rope_tables(seq):
    half = MLA_ROPE_DIM // 2
    pos = jnp.arange(seq, dtype=F32)
    inv_freq = ROPE_THETA ** (-jnp.arange(0, MLA_ROPE_DIM, 2, dtype=F32) / MLA_ROPE_DIM)
    ang = pos[:, None] * inv_freq[None, :]
    cos, sin = jnp.cos(ang), jnp.sin(ang)
    zero = jnp.zeros((seq, half), F32)
    pad = jnp.zeros((seq, LANES - MLA_ROPE_DIM), F32)
    cos_t = jnp.concatenate([cos, cos, pad], axis=1)
    sin_hi = jnp.concatenate([zero, sin, pad], axis=1)
    sin_lo = jnp.concatenate([-sin, zero, pad], axis=1)
    return cos_t, sin_hi, sin_lo


def _layer(x2, cond_in, batch, seq, w_ada, b_ada, g_attn_pre, g_attn_post, w_in, g_q_lat, w_q_b,
           g_kv_lat, w_kv_b, sinks, w_out, g_ffn_pre, g_ffn_post, w_peer_q, sub_keys, peer_u, peer_v):
    t, d = x2.shape
    q_rank = g_q_lat.shape[0]
    kv_rank = g_kv_lat.shape[0]
    n_mla = d // MLA_V_DIM
    kvw = SWA_KV_HEADS * SWA_HEAD_DIM
    heads = sub_keys.shape[0]

    ada = _ada(cond_in, w_ada, b_ada).reshape(batch, N_ADA, 1, d)
    shift_a, scale_a, gate_a, shift_f, scale_f, gate_f = [ada[:, i] for i in range(N_ADA)]

    w_a_cols = q_rank + kv_rank + MLA_ROPE_DIM
    w_a = jnp.pad(w_in[:, :w_a_cols], ((0, 0), (0, -w_a_cols % (2 * LANES)))).astype(BF16)
    off_swa = w_a_cols
    off_gate = off_swa + d + 2 * kvw
    w_swa = w_in[:, off_swa:off_gate].astype(BF16)
    w_g = w_in[:, off_gate:].astype(BF16)
    qk = MLA_NOPE_DIM + MLA_ROPE_DIM
    w_q = jnp.pad(w_q_b.reshape(q_rank, n_mla, qk),
                  ((0, 0), (0, 0), (0, MLA_QK_PAD - qk))).reshape(q_rank, n_mla * MLA_QK_PAD).astype(BF16)
    w_kv = w_kv_b.reshape(kv_rank, n_mla, MLA_NOPE_DIM + MLA_V_DIM)
    w_kn = w_kv[:, :, :MLA_NOPE_DIM].reshape(kv_rank, n_mla * MLA_NOPE_DIM).astype(BF16)
    w_v = w_kv[:, :, MLA_NOPE_DIM:].reshape(kv_rank, n_mla * MLA_V_DIM).astype(BF16)
    rope = _rope_tables(seq)

    h = _prenorm(x2, g_attn_pre, scale_a, shift_a, seq)
    z_a = _matmul(h, w_a, F32, name="in_proj_mla")
    z_swa = _matmul(h, w_swa, F32, name="in_proj_swa")
    z_g = _matmul(h, w_g, F32, name="in_proj_gates")
    q = _mla_q(z_a, g_q_lat, w_q, rope, seq, q_rank, n_mla)
    k, v = _mla_kv(z_a, g_kv_lat, w_kn, w_v, rope, seq, q_rank, kv_rank, n_mla)
    gated_a = _mla_attn(q, k, v, z_g, batch, seq, n_mla)
    mix = _swa(z_swa, z_g, gated_a, sinks, seq, d)
    y = _matmul(mix, w_out.astype(BF16), F32, name="out_proj")
    x1, h2t = _post_attn(x2, y, g_attn_post, gate_a, g_ffn_pre, scale_f, shift_f, seq)

    pqt = _matmul(w_peer_q.T.astype(BF16), h2t, BF16, name="peer_query")
    n1, e1, b2, e2 = _peer_topk(pqt, sub_keys.astype(BF16))
    yt = _peer_dense(h2t, peer_u.astype(BF16), peer_v.astype(BF16), n1, e1, b2, e2, heads)
    return _final(x1, yt, g_ffn_post, gate_f, seq)


def kernel(x, c, w_ada, b_ada, g_attn_pre, g_attn_post, w_in, g_q_lat, w_q_b, g_kv_lat, w_kv_b, sinks, w_out, g_ffn_pre, g_ffn_post, w_peer_q, peer_sub_keys, peer_u, peer_v):
    batch, seq, d = x.shape
    x2 = x.reshape(batch * seq, d)
    for l in range(w_ada.shape[0]):
        x2 = _layer(x2, c, batch, seq, w_ada[l], b_ada[l], g_attn_pre[l], g_attn_post[l], w_in[l],
                    g_q_lat[l], w_q_b[l], g_kv_lat[l], w_kv_b[l], sinks[l], w_out[l], g_ffn_pre[l],
                    g_ffn_post[l], w_peer_q[l], peer_sub_keys[l], peer_u[l], peer_v[l])
    return x2.reshape(batch, seq, d)
```

```python
import functools
import math

import jax
import jax.numpy as jnp
from jax import lax
from jax.experimental import pallas as pl
from jax.experimental.pallas import tpu as pltpu

F32 = jnp.float32
BF16 = jnp.bfloat16

MLA_NOPE_DIM = 128
MLA_ROPE_DIM = 64
MLA_V_DIM = 128
MLA_QK_PAD = 256
MLA_DIAG_STRIPS = 4
ROPE_THETA = 10000.0
SWA_HEAD_DIM = 64
SWA_KV_HEADS = 8
WINDOW = 128
PEER_HEADS = 8
PEER_N_KEYS = 128
PEER_TOPK = 16
N_ADA = 6
NORM_EPS = 1e-6
NEG_INF = -1e30

LANES = 128
VMEM_LIMIT = 56 * 1024 * 1024
VMEM_LIMIT_LARGE = 60 * 1024 * 1024


def _params(sem, vmem=VMEM_LIMIT):
    return pltpu.CompilerParams(dimension_semantics=sem, vmem_limit_bytes=vmem)


def _rms(x, g):
    return x * lax.rsqrt(jnp.mean(x * x, axis=-1, keepdims=True) + NORM_EPS) * g


def _ada_kernel(cb_ref, w_ref, b_ref, o_ref):
    nb = cb_ref.shape[0]
    tn = w_ref.shape[1]
    for b in range(nb):
        cb = cb_ref[b]
        cond = cb * jax.nn.sigmoid(cb)
        for j in range(tn // LANES):
            sl = slice(j * LANES, (j + 1) * LANES)
            r = jnp.sum(w_ref[:, sl] * cond, axis=0, keepdims=True)
            o_ref[b:b + 1, sl] = r + b_ref[:, sl]


def _ada(c, w, bias):
    nb, k = c.shape
    n = w.shape[1]
    tn = min(1024, n)
    cb = jnp.broadcast_to(c[:, :, None], (nb, k, LANES))
    return pl.pallas_call(
        _ada_kernel,
        grid=(n // tn,),
        in_specs=[pl.BlockSpec((nb, k, LANES), lambda j: (0, 0, 0)),
                  pl.BlockSpec((k, tn), lambda j: (0, j)),
                  pl.BlockSpec((1, tn), lambda j: (0, j))],
        out_specs=pl.BlockSpec((nb, tn), lambda j: (0, j)),
        out_shape=jax.ShapeDtypeStruct((nb, n), F32),
        compiler_params=_params(("arbitrary",)),
        name="ada_matvec",
    )(cb, w, bias.reshape(1, n))


def _prenorm_kernel(x_ref, g_ref, sc_ref, sh_ref, o_ref):
    h = _rms(x_ref[...], g_ref[...]) * (1.0 + sc_ref[0]) + sh_ref[0]
    o_ref[...] = h.astype(o_ref.dtype)


def _prenorm(x2, g, scale, shift, seq):
    t, d = x2.shape
    tm = min(512, seq)
    per = seq // tm
    row = pl.BlockSpec((tm, d), lambda i: (i, 0))
    mod = pl.BlockSpec((1, 1, d), lambda i: (i // per, 0, 0))
    return pl.pallas_call(
        _prenorm_kernel,
        grid=(t // tm,),
        in_specs=[row, pl.BlockSpec((1, d), lambda i: (0, 0)), mod, mod],
        out_specs=row,
        out_shape=jax.ShapeDtypeStruct((t, d), BF16),
        compiler_params=_params(("arbitrary",)),
        name="prenorm_modulate",
    )(x2, g.reshape(1, d), scale, shift)


def _post_kernel(x_ref, y_ref, gpost_ref, gate_ref, gpre_ref, sc_ref, sh_ref, x1_ref, h_ref):
    x1 = x_ref[...] + gate_ref[0] * _rms(y_ref[...], gpost_ref[...])
    x1_ref[...] = x1
    h = _rms(x1, gpre_ref[...]) * (1.0 + sc_ref[0]) + sh_ref[0]
    h_ref[...] = h.T.astype(h_ref.dtype)


def _post_attn(x2, y, g_post, gate, g_pre, scale, shift, seq):
    t, d = x2.shape
    tm = min(256, seq)
    per = seq // tm
    row = pl.BlockSpec((tm, d), lambda i: (i, 0))
    vec = pl.BlockSpec((1, d), lambda i: (0, 0))
    mod = pl.BlockSpec((1, 1, d), lambda i: (i // per, 0, 0))
    return pl.pallas_call(
        _post_kernel,
        grid=(t // tm,),
        in_specs=[row, row, vec, mod, vec, mod, mod],
        out_specs=[row, pl.BlockSpec((d, tm), lambda i: (0, i))],
        out_shape=[jax.ShapeDtypeStruct((t, d), F32), jax.ShapeDtypeStruct((d, t), BF16)],
        compiler_params=_params(("arbitrary",)),
        name="post_attn_norm",
    )(x2, y, g_post.reshape(1, d), gate, g_pre.reshape(1, d), scale, shift)


def _final_kernel(x_ref, yt_ref, g_ref, gate_ref, o_ref):
    o_ref[...] = x_ref[...] + gate_ref[0] * _rms(yt_ref[...].T, g_ref[...])


def _final(x1, yt, g, gate, seq):
    t, d = x1.shape
    tm = min(256, seq)
    per = seq // tm
    row = pl.BlockSpec((tm, d), lambda i: (i, 0))
    return pl.pallas_call(
        _final_kernel,
        grid=(t // tm,),
        in_specs=[row, pl.BlockSpec((d, tm), lambda i: (0, i)), pl.BlockSpec((1, d), lambda i: (0, 0)),
                  pl.BlockSpec((1, 1, d), lambda i: (i // per, 0, 0))],
        out_specs=row,
        out_shape=jax.ShapeDtypeStruct((t, d), F32),
        compiler_params=_params(("arbitrary",)),
        name="final_residual",
    )(x1, yt, g.reshape(1, d), gate)


def _mm_kernel(a_ref, w_ref, o_ref):
    o_ref[...] = jnp.dot(a_ref[...], w_ref[...], preferred_element_type=F32).astype(o_ref.dtype)


def _matmul(a, w, out_dtype, name):
    m, k = a.shape
    n = w.shape[1]
    tm = min(1024, m)
    tn = next(c for c in (1024, 896, 768, 640, 512, 384, 256, 128, n) if n % c == 0)
    return pl.pallas_call(
        _mm_kernel,
        grid=(m // tm, n // tn),
        in_specs=[pl.BlockSpec((tm, k), lambda i, j: (i, 0)),
                  pl.BlockSpec((k, tn), lambda i, j: (0, j))],
        out_specs=pl.BlockSpec((tm, tn), lambda i, j: (i, j)),
        out_shape=jax.ShapeDtypeStruct((m, n), out_dtype),
        compiler_params=_params(("arbitrary", "arbitrary")),
        name=name,
    )(a, w)


def _rope_upper(hi, cos_ref, sin_hi_ref, sin_lo_ref):
    return (hi * cos_ref[...]
            + pltpu.roll(hi, 32, axis=1) * sin_hi_ref[...]
            + pltpu.roll(hi, 96, axis=1) * sin_lo_ref[...])


def _mla_q_kernel(z_ref, g_ref, w_ref, cos_ref, s1_ref, s2_ref, o_ref, *, scale, heads):
    half = z_ref.shape[0] // 2
    parts = [slice(0, half), slice(half, 2 * half)]
    qns = [_rms(z_ref[r, :], g_ref[...]).astype(BF16) for r in parts]
    ys = [jnp.dot(qn, w_ref[...], preferred_element_type=F32) for qn in qns]
    for r, y in zip(parts, ys):
        tabs = [cos_ref.at[r, :], s1_ref.at[r, :], s2_ref.at[r, :]]
        for h in range(heads):
            lo = y[:, h * MLA_QK_PAD:h * MLA_QK_PAD + LANES]
            hi = y[:, h * MLA_QK_PAD + LANES:(h + 1) * MLA_QK_PAD]
            o_ref[h * MLA_QK_PAD:h * MLA_QK_PAD + LANES, r] = (lo * scale).T.astype(o_ref.dtype)
            o_ref[h * MLA_QK_PAD + LANES:(h + 1) * MLA_QK_PAD, r] = (
                _rope_upper(hi, *tabs) * scale).T.astype(o_ref.dtype)


def _mla_q(z_a, g_q, w_q, rope, seq, q_rank, n_heads):
    t = z_a.shape[0]
    tm = min(1024, seq)
    hb = min(4, n_heads)
    per = seq // tm
    scale = (MLA_NOPE_DIM + MLA_ROPE_DIM) ** -0.5 * math.log2(math.e)
    tab = pl.BlockSpec((tm, LANES), lambda i, j: (i % per, 0))
    return pl.pallas_call(
        functools.partial(_mla_q_kernel, scale=scale, heads=hb),
        grid=(t // tm, n_heads // hb),
        in_specs=[pl.BlockSpec((tm, q_rank), lambda i, j: (i, 0)),
                  pl.BlockSpec((1, q_rank), lambda i, j: (0, 0)),
                  pl.BlockSpec((q_rank, hb * MLA_QK_PAD), lambda i, j: (0, j)),
                  tab, tab, tab],
        out_specs=pl.BlockSpec((hb * MLA_QK_PAD, tm), lambda i, j: (j, i)),
        out_shape=jax.ShapeDtypeStruct((n_heads * MLA_QK_PAD, t), BF16),
        compiler_params=_params(("arbitrary", "arbitrary")),
        name="mla_q_proj",
    )(z_a, g_q.reshape(1, q_rank), w_q, *rope)


def _mla_kv_kernel(c_ref, pe_ref, g_ref, wk_ref, wv_ref, cos_ref, s1_ref, s2_ref,
                   k_ref, v_ref, *, heads):
    c = _rms(c_ref[...], g_ref[...]).astype(BF16)
    kn = jnp.dot(c, wk_ref[...], preferred_element_type=F32)
    v_ref[...] = jnp.dot(c, wv_ref[...], preferred_element_type=F32).T.astype(v_ref.dtype)
    pe = _rope_upper(pe_ref[...], cos_ref, s1_ref, s2_ref).astype(k_ref.dtype)
    for h in range(heads):
        k_ref[:, h * MLA_QK_PAD:h * MLA_QK_PAD + LANES] = (
            kn[:, h * LANES:(h + 1) * LANES].astype(k_ref.dtype))
        k_ref[:, h * MLA_QK_PAD + LANES:(h + 1) * MLA_QK_PAD] = pe


def _mla_kv(z_a, g_kv, w_kn, w_v, rope, seq, q_rank, kv_rank, n_heads):
    t = z_a.shape[0]
    tm = min(256, seq)
    per = seq // tm
    tk = _attn_block(seq)
    sub = tk // tm
    tab = pl.BlockSpec((tm, LANES), lambda i: (i % per, 0))
    full = lambda shape: pl.BlockSpec(shape, lambda i: (0, 0))
    return pl.pallas_call(
        functools.partial(_mla_kv_kernel, heads=n_heads),
        grid=(t // tm,),
        in_specs=[pl.BlockSpec((tm, kv_rank), lambda i: (i, q_rank // kv_rank)),
                  pl.BlockSpec((tm, LANES), lambda i: (i, (q_rank + kv_rank) // LANES)),
                  full((1, kv_rank)),
                  full((kv_rank, n_heads * MLA_NOPE_DIM)),
                  full((kv_rank, n_heads * MLA_V_DIM)),
                  tab, tab, tab],
        out_specs=[pl.BlockSpec((tm, n_heads * MLA_QK_PAD), lambda i: (i, 0)),
                   pl.BlockSpec((None, n_heads * MLA_V_DIM, tm), lambda i: (i // sub, 0, i % sub))],
        out_shape=[jax.ShapeDtypeStruct((t, n_heads * MLA_QK_PAD), BF16),
                   jax.ShapeDtypeStruct((t // tk, n_heads * MLA_V_DIM, tk), BF16)],
        compiler_params=_params(("arbitrary",)),
        name="mla_kv_proj",
    )(z_a, z_a, g_kv.reshape(1, kv_rank), w_kn, w_v, *rope)


def _attn_block(seq):
    return min(1024, seq)


def _mla_attn_kernel(qt_ref, k_ref, vt_ref, za_ref, o_ref, s_ref, m_ref, l_ref, acc_ref, *, tq, nq):
    def put_scores(slot, qi, j):
        start = j * tq if isinstance(j, int) else pl.multiple_of(j * tq, tq)
        kb = k_ref[pl.ds(start, tq), :]
        s_ref[slot] = jnp.dot(kb, qt_ref[:, qi * tq:(qi + 1) * tq],
                              preferred_element_type=F32)

    def update(s, j, keys=slice(None), qs=slice(None)):
        m = m_ref[:, qs]
        m_new = jnp.maximum(m, jnp.max(s, axis=0, keepdims=True))
        alpha = jnp.exp2(m - m_new)
        p = jnp.exp2(s - m_new)
        acc_ref[:, qs] = alpha * acc_ref[:, qs] + jnp.dot(
            vt_ref[j, :, keys], p.astype(BF16), preferred_element_type=F32)
        l_ref[:, qs] = alpha * l_ref[:, qs] + jnp.sum(p, axis=0, keepdims=True)
        m_ref[:, qs] = m_new

    strip = tq // MLA_DIAG_STRIPS
    diag_parts = tuple((slice(i * strip, (i + 1) * strip), slice(i * strip, tq))
                       for i in range(MLA_DIAG_STRIPS))

    def put_diag(slot, qi):
        for keys, qs in diag_parts:
            s_ref[slot, keys, qs] = jnp.dot(
                k_ref[qi * tq + keys.start:qi * tq + keys.stop, :],
                qt_ref[:, qi * tq + qs.start:qi * tq + qs.stop], preferred_element_type=F32)

    def finish(slot, qi):
        for keys, qs in diag_parts:
            shape = (strip, qs.stop - qs.start)
            key = lax.broadcasted_iota(jnp.int32, shape, 0) + (keys.start - qs.start)
            qry = lax.broadcasted_iota(jnp.int32, shape, 1)
            update(jnp.where(key <= qry, s_ref[slot, keys, qs], NEG_INF), qi, keys, qs)
        rows = slice(qi * tq, (qi + 1) * tq)
        o = (acc_ref[...] / l_ref[...]).T
        o_ref[rows, :] = (o * jax.nn.sigmoid(za_ref[rows, :])).astype(o_ref.dtype)

    put_diag(0, 0)
    first = 0
    for qi in range(nq):
        a, b = first, 1 - first
        m_ref[...] = jnp.full(m_ref.shape, NEG_INF, F32)
        l_ref[...] = jnp.zeros(l_ref.shape, F32)
        acc_ref[...] = jnp.zeros(acc_ref.shape, F32)

        def pair(jj, _, qi=qi, a=a, b=b):
            j = 2 * jj
            put_scores(b, qi, j + 1)
            update(s_ref[a], j)
            put_scores(a, qi, j + 2)
            update(s_ref[b], j + 1)
            return 0

        if qi % 2 == 1:
            if qi >= 3:
                lax.fori_loop(0, qi // 2, pair, 0)
            put_diag(b, qi)
            update(s_ref[a], qi - 1)
            diag, free = b, a
        else:
            if qi >= 4:
                lax.fori_loop(0, qi // 2 - 1, pair, 0)
            if qi >= 2:
                put_scores(b, qi, qi - 1)
                update(s_ref[a], qi - 2)
                put_diag(a, qi)
                update(s_ref[b], qi - 1)
            diag, free = a, b
        if qi + 1 < nq:
            put_scores(free, qi + 1, 0)
        finish(diag, qi)
        first = free


def _mla_attn(qt, k, vt, z_g, batch, seq, n_heads):
    t = k.shape[0]
    tq = _attn_block(seq)
    nq = seq // tq
    return pl.pallas_call(
        functools.partial(_mla_attn_kernel, tq=tq, nq=nq),
        grid=(batch, n_heads),
        in_specs=[pl.BlockSpec((MLA_QK_PAD, seq), lambda b, h: (h, b)),
                  pl.BlockSpec((seq, MLA_QK_PAD), lambda b, h: (b, h)),
                  pl.BlockSpec((nq, MLA_V_DIM, tq), lambda b, h: (b, h, 0)),
                  pl.BlockSpec((seq, MLA_V_DIM), lambda b, h: (b, h))],
        out_specs=pl.BlockSpec((seq, MLA_V_DIM), lambda b, h: (b, h)),
        out_shape=jax.ShapeDtypeStruct((t, n_heads * MLA_V_DIM), BF16),
        scratch_shapes=[pltpu.VMEM((2, tq, tq), F32), pltpu.VMEM((1, tq), F32),
                        pltpu.VMEM((1, tq), F32), pltpu.VMEM((MLA_V_DIM, tq), F32)],
        compiler_params=_params(("arbitrary", "arbitrary")),
        name="mla_flash_attention",
    )(qt, k, vt, z_g)


def _swa_bias(n_heads):
    key = jnp.arange(2 * WINDOW)[:, None]
    qry = jnp.arange(WINDOW)[None, :]
    dist = qry + WINDOW - key
    band = (dist >= 0) & (dist < WINDOW)
    slopes = jnp.asarray([2.0 ** (-8.0 * (h + 1) / n_heads) * math.log2(math.e)
                          for h in range(n_heads)], F32)
    bias = jnp.where(band, -slopes[:, None, None] * dist.astype(F32), NEG_INF)
    return jnp.stack([bias, jnp.where(key >= WINDOW, bias, NEG_INF)])


def _swa_kernel(sink_ref, bias_ref, q_ref, kp_ref, kc_ref, vp_ref, vc_ref, zb_ref, a_ref, o_ref,
                *, n_heads):
    group = n_heads // SWA_KV_HEADS
    hd = SWA_HEAD_DIM
    log2e = math.log2(math.e)
    scale = hd ** -0.5 * log2e
    nt = (((1,), (1,)), ((), ()))
    tn = (((0,), (0,)), ((), ()))
    for n in range(SWA_KV_HEADS):
        ksl = slice(n * hd, (n + 1) * hd)
        kb = jnp.concatenate([kp_ref[:, ksl], kc_ref[:, ksl]], axis=0).astype(BF16)
        vb = jnp.concatenate([vp_ref[:, ksl], vc_ref[:, ksl]], axis=0).astype(BF16)
        heads = range(n * group, (n + 1) * group)
        qs = [(q_ref[:, hq * hd:(hq + 1) * hd] * scale).astype(BF16) for hq in heads]
        ss = [lax.dot_general(kb, qh, nt, preferred_element_type=F32) for qh in qs]
        ps, dens = [], []
        for hq, s in zip(heads, ss):
            sink = sink_ref[hq] * log2e
            s = s + bias_ref[hq]
            m = jnp.maximum(jnp.max(s, axis=0, keepdims=True), sink)
            p = jnp.exp2(s - m)
            dens.append(jnp.sum(p, axis=0, keepdims=True) + jnp.exp2(sink - m))
            ps.append(p.astype(BF16))
        outs = [lax.dot_general(vb, p, tn, preferred_element_type=F32) / den
                for p, den in zip(ps, dens)]
        for g in range(0, group, 2):
            hq = n * group + g
            sl = slice(hq * hd, (hq + 2) * hd)
            o2 = jnp.concatenate([outs[g], outs[g + 1]], axis=0).T
            mix = a_ref[:, sl].astype(F32) + jax.nn.sigmoid(zb_ref[:, sl]) * o2
            o_ref[:, sl] = mix.astype(o_ref.dtype)


def _swa(z_swa, z_g, gated_a, sinks, seq, d):
    t = z_swa.shape[0]
    n_heads = d // SWA_HEAD_DIM
    kvw = SWA_KV_HEADS * SWA_HEAD_DIM
    per = seq // WINDOW
    kcol = d // kvw
    prev = lambda c: (lambda i: (jnp.maximum(i - 1, 0), c))
    cur = lambda c: (lambda i: (i, c))
    return pl.pallas_call(
        functools.partial(_swa_kernel, n_heads=n_heads),
        grid=(t // WINDOW,),
        in_specs=[pl.BlockSpec(memory_space=pltpu.SMEM),
                  pl.BlockSpec((None, n_heads, 2 * WINDOW, WINDOW),
                               lambda i: (jnp.where(i % per == 0, 1, 0), 0, 0, 0)),
                  pl.BlockSpec((WINDOW, d), cur(0)),
                  pl.BlockSpec((WINDOW, kvw), prev(kcol)),
                  pl.BlockSpec((WINDOW, kvw), cur(kcol)),
                  pl.BlockSpec((WINDOW, kvw), prev(kcol + 1)),
                  pl.BlockSpec((WINDOW, kvw), cur(kcol + 1)),
                  pl.BlockSpec((WINDOW, d), cur(1)),
                  pl.BlockSpec((WINDOW, d), cur(0))],
        out_specs=pl.BlockSpec((WINDOW, d), cur(0)),
        out_shape=jax.ShapeDtypeStruct((t, d), BF16),
        compiler_params=_params(("arbitrary",)),
        name="swa_sink_attention",
    )(sinks, _swa_bias(n_heads), z_swa, z_swa, z_swa, z_swa, z_swa, z_g, gated_a)


def _top16(ss, exact):
    ss = list(ss)
    rows = lax.broadcasted_iota(jnp.int32, ss[0].shape, 0).astype(F32)
    ranks = [jnp.full(s.shape, float(PEER_TOPK), F32) for s in ss]
    vals = [[] for _ in ss]
    for r in range(PEER_TOPK):
        for i, s in enumerate(ss):
            m = jnp.max(s, axis=0, keepdims=True)
            if exact:
                first = jnp.min(jnp.where(s == m, rows, float(PEER_N_KEYS)), axis=0, keepdims=True)
                hit = rows == first
            else:
                hit = s == m
            ranks[i] = jnp.where(hit, float(r), ranks[i])
            ss[i] = jnp.where(hit, NEG_INF, s)
            vals[i].append(m)
    return list(zip(vals, ranks))


def _peer_select(s1, s2, exact):
    k = PEER_TOPK
    (v1, rank1), (v2, rank2) = _top16([s1, s2], exact)
    n = s1.shape[1]
    hk = k // 2
    r8 = lax.broadcasted_iota(jnp.int32, (hk, n), 0).astype(F32)
    colsum = lambda x: jnp.sum(x, axis=0, keepdims=True)

    def stack(vals):
        out = jnp.zeros((hk, n), F32)
        for i, v in enumerate(vals):
            out = jnp.where(r8 == float(i), v, out)
        return out

    v2_lo, v2_hi, v1_hi = stack(v2[:hk]), stack(v2[hk:]), stack(v1[hk:])
    cand = [v1[0] + v2_lo, v1[0] + v2_hi]
    code = [r8, r8 + float(hk)]
    for a in range(1, hk):
        cand.append(jnp.where(r8 < float(k // (a + 1)), v1[a] + v2_lo, NEG_INF))
        code.append(r8 + float(a * k))
    cand.append(v1_hi + v2[0])
    code.append((r8 + float(hk)) * float(k))
    sel = [jnp.zeros((hk, n), F32) for _ in cand]
    for _ in range(k):
        m = jnp.max(functools.reduce(jnp.maximum, cand), axis=0, keepdims=True)
        if exact:
            first = functools.reduce(
                jnp.minimum, [jnp.where(cv == m, cd, float(k * k)) for cv, cd in zip(cand, code)])
            first = jnp.min(first, axis=0, keepdims=True)
        for i in range(len(cand)):
            hit = (code[i] == first) if exact else (cand[i] == m)
            sel[i] = jnp.where(hit, 1.0, sel[i])
            cand[i] = jnp.where(hit, NEG_INF, cand[i])
    e2_lo = jnp.exp(v2_lo - v2[0])
    e2_hi = jnp.exp(v2_hi - v2[0])
    cnt = [colsum(sel[0]) + colsum(sel[1])] + [colsum(sel[a + 1]) for a in range(1, hk)]
    mass = [colsum(sel[0] * e2_lo) + colsum(sel[1] * e2_hi)]
    mass += [colsum(sel[a + 1] * e2_lo) for a in range(1, hk)]
    z = jnp.zeros((1, n), F32)
    n1 = jnp.zeros(s1.shape, F32)
    for a in range(hk):
        z = z + jnp.exp(v1[a] - v1[0]) * mass[a]
        n1 = jnp.where(rank1 == float(a), cnt[a], n1)
    z = z + colsum(sel[-1] * jnp.exp(v1_hi - v1[0]))
    for a in range(hk, k):
        n1 = jnp.where(rank1 == float(a), sel[-1][a - hk:a - hk + 1, :], n1)
    e1 = jnp.exp(s1 - v1[0])
    e2 = jnp.exp(s2 - v2[0]) / z
    if exact:
        bad = jnp.zeros((1, n), F32)
    else:
        ranked = lambda rk: colsum(jnp.where(rk < float(k), 1.0, 0.0))
        picked = functools.reduce(lambda x, y: x + y, cnt) + colsum(sel[-1])
        bad = (jnp.abs(ranked(rank1) - float(k)) + jnp.abs(ranked(rank2) - float(k))
               + jnp.abs(picked - float(k)))
    return (n1, e1, rank2, e2), bad


def _peer_topk_kernel(q_ref, key_ref, n1_ref, e1_ref, b2_ref, e2_ref, *, sub):
    tm = q_ref.shape[1]
    half = q_ref.shape[0] // 2

    def scores(cs):
        s1 = jnp.dot(key_ref[0, 0], q_ref[:half, cs], preferred_element_type=F32)
        s2 = jnp.dot(key_ref[0, 1], q_ref[half:, cs], preferred_element_type=F32)
        return s1, s2

    def store(cs, res):
        n1, e1, b2, e2 = res
        n1_ref[:, cs] = n1
        e1_ref[:, cs] = e1
        b2_ref[:, cs] = b2.astype(b2_ref.dtype)
        e2_ref[:, cs] = e2.astype(e2_ref.dtype)

    groups = [slice(c * sub, (c + 1) * sub) for c in range(tm // sub)]
    bad = None
    for cs in groups:
        res, flag = _peer_select(*scores(cs), exact=False)
        store(cs, res)
        bad = flag if bad is None else jnp.maximum(bad, flag)

    @pl.when(jnp.max(bad) > 0.0)
    def _():
        for cs in groups:
            store(cs, _peer_select(*scores(cs), exact=True)[0])


def _peer_topk(qt, keys):
    t = qt.shape[1]
    heads = keys.shape[0]
    qd = qt.shape[0] // heads
    tm = min(512, t)
    out = pl.BlockSpec((PEER_N_KEYS, tm), lambda i, h: (h, i))
    shape = lambda dt: jax.ShapeDtypeStruct((heads * PEER_N_KEYS, t), dt)
    return pl.pallas_call(
        functools.partial(_peer_topk_kernel, sub=min(256, tm)),
        grid=(t // tm, heads),
        in_specs=[pl.BlockSpec((qd, tm), lambda i, h: (h, i)),
                  pl.BlockSpec((1, 2, PEER_N_KEYS, qd // 2), lambda i, h: (h, 0, 0, 0))],
        out_specs=[out, out, out, out],
        out_shape=[shape(F32), shape(F32), shape(BF16), shape(BF16)],
        compiler_params=_params(("arbitrary", "arbitrary")),
        name="peer_topk",
    )(qt, keys)


def _peer_dense_kernel(ht_ref, u_ref, v_ref, b2_ref, e2_ref, *rest, heads):
    n1_refs, e1_refs = rest[:heads], rest[heads:2 * heads]
    o_ref, a_ref = rest[2 * heads:]
    c = pl.program_id(1)
    tn = u_ref.shape[0]
    nk = PEER_N_KEYS

    @pl.when(c == 0)
    def _():
        o_ref[...] = jnp.zeros_like(o_ref)

    hid = jnp.dot(u_ref[...], ht_ref[...], preferred_element_type=F32)
    act = (0.5 * hid * (1.0 + lax.erf(hid * (2.0 ** -0.5)))).astype(BF16)
    for k in range(tn // nk):
        gate = None
        for h in range(heads):
            n1 = n1_refs[h][k:k + 1, :].astype(BF16)
            e1 = e1_refs[h][k:k + 1, :].astype(BF16)
            b2 = b2_ref[h * nk:(h + 1) * nk, :]
            e2 = e2_ref[h * nk:(h + 1) * nk, :]
            term = jnp.where(b2 < n1, e2, jnp.zeros_like(e2)) * e1
            gate = term if gate is None else gate + term
        a_ref[k * nk:(k + 1) * nk, :] = act[k * nk:(k + 1) * nk, :] * gate
    o_ref[...] += lax.dot_general(v_ref[...], a_ref[...], (((0,), (0,)), ((), ())),
                                  preferred_element_type=F32)


def _peer_dense(ht, u, v, n1, e1, b2, e2, heads):
    d, t = ht.shape
    n_exp = u.shape[0]
    tm = min(512, t)
    tn = min(1024, n_exp)
    rows = heads * PEER_N_KEYS
    per_chunk = tn // PEER_N_KEYS
    chunks_per_head = PEER_N_KEYS // per_chunk
    once = pl.Buffered(1)
    sel = pl.BlockSpec((rows, tm), lambda i, c: (0, i), pipeline_mode=once)
    head_rows = [pl.BlockSpec((per_chunk, tm), lambda i, c, h=h: (h * chunks_per_head + c, i))
                 for h in range(heads)]
    return pl.pallas_call(
        functools.partial(_peer_dense_kernel, heads=heads),
        grid=(t // tm, n_exp // tn),
        in_specs=[pl.BlockSpec((d, tm), lambda i, c: (0, i), pipeline_mode=once),
                  pl.BlockSpec((tn, d), lambda i, c: (c, 0)),
                  pl.BlockSpec((tn, d), lambda i, c: (c, 0)),
                  sel, sel] + head_rows + head_rows,
        out_specs=pl.BlockSpec((d, tm), lambda i, c: (0, i), pipeline_mode=once),
        out_shape=jax.ShapeDtypeStruct((d, t), F32),
        scratch_shapes=[pltpu.VMEM((tn, tm), BF16)],
        compiler_params=_params(("arbitrary", "arbitrary"), vmem=VMEM_LIMIT_LARGE),
        name="peer_dense_experts",
    )(ht, u, v, b2, e2, *([n1] * heads), *([e1] * heads))


def _rope_tables(seq):
    half = MLA_ROPE_DIM // 2
    pos = jnp.arange(seq, dtype=F32)
    inv_freq = ROPE_THETA ** (-jnp.arange(0, MLA_ROPE_DIM, 2, dtype=F32) / MLA_ROPE_DIM)
    ang = pos[:, None] * inv_freq[None, :]
    cos, sin = jnp.cos(ang), jnp.sin(ang)
    zero = jnp.zeros((seq, half), F32)
    pad = jnp.zeros((seq, LANES - MLA_ROPE_DIM), F32)
    cos_t = jnp.concatenate([cos, cos, pad], axis=1)
    sin_hi = jnp.concatenate([zero, sin, pad], axis=1)
    sin_lo = jnp.concatenate([-sin, zero, pad], axis=1)
    return cos_t, sin_hi, sin_lo


def _layer(x2, cond_in, batch, seq, w_ada, b_ada, g_attn_pre, g_attn_post, w_in, g_q_lat, w_q_b,
           g_kv_lat, w_kv_b, sinks, w_out, g_ffn_pre, g_ffn_post, w_peer_q, sub_keys, peer_u, peer_v):
    t, d = x2.shape
    q_rank = g_q_lat.shape[0]
    kv_rank = g_kv_lat.shape[0]
    n_mla = d // MLA_V_DIM
    kvw = SWA_KV_HEADS * SWA_HEAD_DIM
    heads = sub_keys.shape[0]

    ada = _ada(cond_in, w_ada, b_ada).reshape(batch, N_ADA, 1, d)
    shift_a, scale_a, gate_a, shift_f, scale_f, gate_f = [ada[:, i] for i in range(N_ADA)]

    w_a_cols = q_rank + kv_rank + MLA_ROPE_DIM
    w_a = jnp.pad(w_in[:, :w_a_cols], ((0, 0), (0, -w_a_cols % (2 * LANES)))).astype(BF16)
    off_swa = w_a_cols
    off_gate = off_swa + d + 2 * kvw
    w_swa = w_in[:, off_swa:off_gate].astype(BF16)
    w_g = w_in[:, off_gate:].astype(BF16)
    qk = MLA_NOPE_DIM + MLA_ROPE_DIM
    w_q = jnp.pad(w_q_b.reshape(q_rank, n_mla, qk),
                  ((0, 0), (0, 0), (0, MLA_QK_PAD - qk))).reshape(q_rank, n_mla * MLA_QK_PAD).astype(BF16)
    w_kv = w_kv_b.reshape(kv_rank, n_mla, MLA_NOPE_DIM + MLA_V_DIM)
    w_kn = w_kv[:, :, :MLA_NOPE_DIM].reshape(kv_rank, n_mla * MLA_NOPE_DIM).astype(BF16)
    w_v = w_kv[:, :, MLA_NOPE_DIM:].reshape(kv_rank, n_mla * MLA_V_DIM).astype(BF16)
    rope = _rope_tables(seq)

    h = _prenorm(x2, g_attn_pre, scale_a, shift_a, seq)
    z_a = _matmul(h, w_a, F32, name="in_proj_mla")
    z_swa = _matmul(h, w_swa, F32, name="in_proj_swa")
    z_g = _matmul(h, w_g, F32, name="in_proj_gates")
    q = _mla_q(z_a, g_q_lat, w_q, rope, seq, q_rank, n_mla)
    k, v = _mla_kv(z_a, g_kv_lat, w_kn, w_v, rope, seq, q_rank, kv_rank, n_mla)
    gated_a = _mla_attn(q, k, v, z_g, batch, seq, n_mla)
    mix = _swa(z_swa, z_g, gated_a, sinks, seq, d)
    y = _matmul(mix, w_out.astype(BF16), F32, name="out_proj")
    x1, h2t = _post_attn(x2, y, g_attn_post, gate_a, g_ffn_pre, scale_f, shift_f, seq)

    pqt = _matmul(w_peer_q.T.astype(BF16), h2t, BF16, name="peer_query")
    n1, e1, b2, e2 = _peer_topk(pqt, sub_keys.astype(BF16))
    yt = _peer_dense(h2t, peer_u.astype(BF16), peer_v.astype(BF16), n1, e1, b2, e2, heads)
    return _final(x1, yt, g_ffn_post, gate_f, seq)


def kernel(x, c, w_ada, b_ada, g_attn_pre, g_attn_post, w_in, g_q_lat, w_q_b, g_kv_lat, w_kv_b, sinks, w_out, g_ffn_pre, g_ffn_post, w_peer_q, peer_sub_keys, peer_u, peer_v):
    batch, seq, d = x.shape
    x2 = x.reshape(batch * seq, d)
    for l in range(w_ada.shape[0]):
        x2 = _layer(x2, c, batch, seq, w_ada[l], b_ada[l], g_attn_pre[l], g_attn_post[l], w_in[l],
                    g_q_lat[l], w_q_b[l], g_kv_lat[l], w_kv_b[l], sinks[l], w_out[l], g_ffn_pre[l],
                    g_ffn_post[l], w_peer_q[l], peer_sub_keys[l], peer_u[l], peer_v[l])
    return x2.reshape(batch, seq, d)
```
